```python
import jax, jax.numpy as jnp
from jax import lax
import numpy as np

D_MODEL = 2048
BATCH = 4
SEQ = 4096
DEPTH = 1

CHUNK = 64
HEAD_DIM = 128
N_HEADS_SB = 8
N_HEADS_CA = 8
SB_WIDTH = N_HEADS_SB * HEAD_DIM
CA_WIDTH = N_HEADS_CA * HEAD_DIM
N_BRANCH = 2
Q_BLOCK = 128
N_PAST_CHUNKS = 8
BAND = (N_PAST_CHUNKS + 1) * CHUNK
REL_CLIP_PAST = 128
N_REL = REL_CLIP_PAST + CHUNK
D_FF = 5632
IN_COLS = 3 * SB_WIDTH + 3 * CA_WIDTH + N_BRANCH * D_MODEL
EPS = 1e-6

kernel_name = "hybrid_stickbreak_chunkrel_macaron_block"


def _rmsnorm(x, gain):
    xf = x.astype(jnp.float32)
    y = xf * lax.rsqrt(jnp.mean(xf * xf, axis=-1, keepdims=True) + EPS)
    return (y * gain.astype(jnp.float32)).astype(x.dtype)


def _swiglu(x, w_gate, w_up, w_down):
    return (jax.nn.silu(x @ w_gate) * (x @ w_up)) @ w_down


def _split_heads(t, n_heads):
    b, s, _ = t.shape
    return t.reshape(b, s, n_heads, HEAD_DIM).transpose(0, 2, 1, 3)


def _merge_heads(t):
    b, h, s, dh = t.shape
    return t.transpose(0, 2, 1, 3).reshape(b, s, h * dh)


def _stick_breaking_attention(q, k, v):
    seq = q.shape[2]
    scale = HEAD_DIM ** -0.5
    outs = []
    for blk in range(seq // Q_BLOCK):
        q0 = blk * Q_BLOCK
        kend = q0 + Q_BLOCK
        qb = q[:, :, q0:kend]
        kb = k[:, :, :kend]
        vb = v[:, :, :kend]
        z = jnp.einsum('bhqd,bhkd->bhqk', qb, kb).astype(jnp.float32) * scale
        t_pos = q0 + np.arange(Q_BLOCK)[:, None]
        s_pos = np.arange(kend)[None, :]
        causal = s_pos < t_pos
        log_keep = jnp.where(causal, jax.nn.log_sigmoid(-z), 0.0)
        later = lax.cumsum(log_keep, axis=3, reverse=True) - log_keep
        log_w = jax.nn.log_sigmoid(z) + later
        w = jnp.where(causal, jnp.exp(log_w), 0.0)
        outs.append(jnp.einsum('bhqk,bhkd->bhqd', w.astype(v.dtype), vb))
    return jnp.concatenate(outs, axis=2)


def _chunked_relpos_attention(q, k, v, rel_bias):
    b, h, seq, dh = q.shape
    nc = seq // CHUNK
    scale = HEAD_DIM ** -0.5
    qc = q.reshape(b, h, nc, CHUNK, dh)
    pad = ((0, 0), (0, 0), (N_PAST_CHUNKS * CHUNK, 0), (0, 0))
    kp = jnp.pad(k, pad).reshape(b, h, nc + N_PAST_CHUNKS, CHUNK, dh)
    vp = jnp.pad(v, pad).reshape(b, h, nc + N_PAST_CHUNKS, CHUNK, dh)
    band_idx = np.arange(nc)[:, None] + np.arange(N_PAST_CHUNKS + 1)[None, :]
    kband = kp[:, :, band_idx].reshape(b, h, nc, BAND, dh)
    vband = vp[:, :, band_idx].reshape(b, h, nc, BAND, dh)
    scores = jnp.einsum('bhcqd,bhckd->bhcqk', qc, kband).astype(jnp.float32) * scale
    dist = np.arange(CHUNK)[:, None] + N_PAST_CHUNKS * CHUNK - np.arange(BAND)[None, :]
    rel_idx = np.clip(dist, -(CHUNK - 1), REL_CLIP_PAST) + (CHUNK - 1)
    bias = rel_bias.astype(jnp.float32)[:, rel_idx]
    scores = scores + bias[None, :, None]
    src_chunk = np.arange(nc)[:, None] - N_PAST_CHUNKS + np.arange(BAND)[None, :] // CHUNK
    valid = src_chunk >= 0
    scores = jnp.where(valid[None, None, :, None, :], scores, -jnp.inf)
    p = jax.nn.softmax(scores, axis=-1)
    out = jnp.einsum('bhcqk,bhckd->bhcqd', p.astype(v.dtype), vband)
    return out.reshape(b, h, seq, dh)


def setup_inputs(seed: int = 0) -> dict:
    key = jax.random.key(seed)
    ks = jax.random.split(key, 20)
    nrm = jax.random.normal
    f32 = jnp.float32

    def gain(k, shape):
        return 1.0 + 0.02 * nrm(k, shape, f32)

    return {
        "x": nrm(ks[0], (BATCH, SEQ, D_MODEL), f32),
        "ffn1_norm": gain(ks[1], (DEPTH, D_MODEL)),
        "ffn1_w_gate": nrm(ks[2], (DEPTH, D_MODEL, D_FF), f32) * D_MODEL ** -0.5,
        "ffn1_w_up": nrm(ks[3], (DEPTH, D_MODEL, D_FF), f32) * D_MODEL ** -0.5,
        "ffn1_w_down": nrm(ks[4], (DEPTH, D_FF, D_MODEL), f32) * D_FF ** -0.5,
        "mix_norm": gain(ks[5], (DEPTH, D_MODEL)),
        "w_in": nrm(ks[6], (DEPTH, D_MODEL, IN_COLS), f32) * D_MODEL ** -0.5,
        "b_gate": 0.05 * nrm(ks[7], (DEPTH, N_BRANCH * D_MODEL), f32),
        "q_norm_ca": gain(ks[8], (DEPTH, HEAD_DIM)),
        "k_norm_ca": gain(ks[9], (DEPTH, HEAD_DIM)),
        "rel_bias": 0.2 * nrm(ks[10], (DEPTH, N_HEADS_CA, N_REL), f32),
        "w_o_sb": nrm(ks[11], (DEPTH, SB_WIDTH, D_MODEL), f32) * SB_WIDTH ** -0.5,
        "w_o_ca": nrm(ks[12], (DEPTH, CA_WIDTH, D_MODEL), f32) * CA_WIDTH ** -0.5,
        "w_out": nrm(ks[13], (DEPTH, D_MODEL, D_MODEL), f32) * D_MODEL ** -0.5,
        "ffn2_norm": gain(ks[14], (DEPTH, D_MODEL)),
        "ffn2_w_gate": nrm(ks[15], (DEPTH, D_MODEL, D_FF), f32) * D_MODEL ** -0.5,
        "ffn2_w_up": nrm(ks[16], (DEPTH, D_MODEL, D_FF), f32) * D_MODEL ** -0.5,
        "ffn2_w_down": nrm(ks[17], (DEPTH, D_FF, D_MODEL), f32) * D_FF ** -0.5,
        "final_norm": gain(ks[18], (DEPTH, D_MODEL)),
    }


def reference(x, ffn1_norm, ffn1_w_gate, ffn1_w_up, ffn1_w_down, mix_norm, w_in, b_gate,
              q_norm_ca, k_norm_ca, rel_bias, w_o_sb, w_o_ca, w_out, ffn2_norm,
              ffn2_w_gate, ffn2_w_up, ffn2_w_down, final_norm):
    b, s, _ = x.shape
    split_at = [SB_WIDTH, 2 * SB_WIDTH, 3 * SB_WIDTH,
                3 * SB_WIDTH + CA_WIDTH, 3 * SB_WIDTH + 2 * CA_WIDTH, 3 * SB_WIDTH + 3 * CA_WIDTH]
    for l in range(DEPTH):
        x = x + 0.5 * _swiglu(_rmsnorm(x, ffn1_norm[l]), ffn1_w_gate[l], ffn1_w_up[l], ffn1_w_down[l])

        hn = _rmsnorm(x, mix_norm[l])
        proj = hn @ w_in[l]
        q_sb, k_sb, v_sb, q_ca, k_ca, v_ca, gate_pre = jnp.split(proj, split_at, axis=-1)

        y_sb = _stick_breaking_attention(_split_heads(q_sb, N_HEADS_SB),
                                         _split_heads(k_sb, N_HEADS_SB),
                                         _split_heads(v_sb, N_HEADS_SB))
        y_sb = _merge_heads(y_sb) @ w_o_sb[l]

        qh = _rmsnorm(_split_heads(q_ca, N_HEADS_CA), q_norm_ca[l])
        kh = _rmsnorm(_split_heads(k_ca, N_HEADS_CA), k_norm_ca[l])
        y_ca = _chunked_relpos_attention(qh, kh, _split_heads(v_ca, N_HEADS_CA), rel_bias[l])
        y_ca = _merge_heads(y_ca) @ w_o_ca[l]

        gates = jax.nn.sigmoid((gate_pre + b_gate[l]).astype(jnp.float32))
        gates = gates.reshape(b, s, N_BRANCH, D_MODEL).astype(x.dtype)
        merged = gates[:, :, 0] * y_sb + gates[:, :, 1] * y_ca
        x = x + merged @ w_out[l]

        x = x + 0.5 * _swiglu(_rmsnorm(x, ffn2_norm[l]), ffn2_w_gate[l], ffn2_w_up[l], ffn2_w_down[l])

        x = _rmsnorm(x, final_norm[l])
    return x
```

```python
import functools

import jax
import jax.numpy as jnp
from jax import lax
from jax.experimental import pallas as pl
from jax.experimental.pallas import tpu as pltpu

D_MODEL = 2048
D_FF = 5632
HEAD_DIM = 128
N_HEADS = 8
WIDTH = N_HEADS * HEAD_DIM
CHUNK = 64
N_PAST_CHUNKS = 8
REL_CLIP_PAST = 128
N_BRANCH = 2
IN_COLS = 6 * WIDTH + N_BRANCH * D_MODEL
EPS = 1e-6
SCALE = HEAD_DIM ** -0.5

COL_Q_SB, COL_K_SB, COL_V_SB, COL_Q_CA, COL_K_CA, COL_V_CA, COL_GATE = range(7)

V7X_VMEM_BYTES = 64 * 1024 * 1024
VMEM_LIMIT = V7X_VMEM_BYTES - 8 * 1024 * 1024

LOG_F32_UNDERFLOW = -104.0
MASK_BIAS = -1e30

F32 = jnp.float32
BF16 = jnp.bfloat16


def _dot(a, b):
    return jnp.dot(a, b, preferred_element_type=F32)


def _dot_nt(a, b):
    return lax.dot_general(a, b, (((1,), (1,)), ((), ())), preferred_element_type=F32)


def _rmsnorm_f32(x, gain):
    return x * lax.rsqrt(jnp.mean(x * x, axis=-1, keepdims=True) + EPS) * gain


def _ffn_kernel(x_ref, nin_ref, wg_ref, wu_ref, wd_ref, nout_ref, *rest, emit_residual):
    if emit_residual:
        o_ref, hn_out_ref, hn_scr = rest
    else:
        o_ref, hn_scr = rest
    f = pl.program_id(1)

    @pl.when(f == 0)
    def _():
        x = x_ref[...]
        hn_scr[...] = _rmsnorm_f32(x, nin_ref[...]).astype(BF16)
        o_ref[...] = x

    h = hn_scr[...]
    g = _dot(h, wg_ref[...])
    u = _dot(h, wu_ref[...])
    a = (g * u * 0.5) / (1.0 + jnp.exp(-g))
    o_ref[...] += _dot(a.astype(BF16), wd_ref[...])

    @pl.when(f == pl.num_programs(1) - 1)
    def _():
        y = _rmsnorm_f32(o_ref[...], nout_ref[...])
        if emit_residual:
            hn_out_ref[...] = y.astype(BF16)
        else:
            o_ref[...] = y


def _ffn(x, norm_in, wg, wu, wd, norm_out, *, emit_residual, tm=512, tf=512):
    m = x.shape[0]
    grid = (m // tm, D_FF // tf)
    out_shape = [jax.ShapeDtypeStruct((m, D_MODEL), F32)]
    out_specs = [pl.BlockSpec((tm, D_MODEL), lambda i, f: (i, 0))]
    if emit_residual:
        out_shape.append(jax.ShapeDtypeStruct((m, D_MODEL), BF16))
        out_specs.append(pl.BlockSpec((tm, D_MODEL), lambda i, f: (i, 0)))
    res = pl.pallas_call(
        functools.partial(_ffn_kernel, emit_residual=emit_residual),
        grid=grid,
        in_specs=[
            pl.BlockSpec((tm, D_MODEL), lambda i, f: (i, 0)),
            pl.BlockSpec((1, D_MODEL), lambda i, f: (0, 0)),
            pl.BlockSpec((D_MODEL, tf), lambda i, f: (0, f)),
            pl.BlockSpec((D_MODEL, tf), lambda i, f: (0, f)),
            pl.BlockSpec((tf, D_MODEL), lambda i, f: (f, 0)),
            pl.BlockSpec((1, D_MODEL), lambda i, f: (0, 0)),
        ],
        out_specs=out_specs,
        out_shape=out_shape,
        scratch_shapes=[pltpu.VMEM((tm, D_MODEL), BF16)],
        compiler_params=pltpu.CompilerParams(
            dimension_semantics=("parallel", "arbitrary"), vmem_limit_bytes=VMEM_LIMIT),
        name="ffn_res" if emit_residual else "ffn_final",
    )(x, norm_in, wg, wu, wd, norm_out)
    return res


def _inproj_kernel(h_ref, w_ref, bg_ref, qn_ref, kn_ref, o_ref):
    j = pl.program_id(1)
    r = _dot(h_ref[...], w_ref[...])

    @pl.when((j <= COL_V_SB) | (j == COL_V_CA))
    def _():
        o_ref[...] = r.astype(BF16)

    def head_norm(gain_ref):
        for hd in range(N_HEADS):
            sl = slice(hd * HEAD_DIM, (hd + 1) * HEAD_DIM)
            o_ref[:, sl] = _rmsnorm_f32(r[:, sl], gain_ref[...]).astype(BF16)

    @pl.when(j == COL_Q_CA)
    def _():
        head_norm(qn_ref)

    @pl.when(j == COL_K_CA)
    def _():
        head_norm(kn_ref)

    @pl.when(j >= COL_GATE)
    def _():
        o_ref[...] = (1.0 / (1.0 + jnp.exp(-(r + bg_ref[...])))).astype(BF16)


def _inproj(hn, w_in, b_gate, q_norm, k_norm, *, tm=1024):
    m = hn.shape[0]
    n_col_blocks = IN_COLS // WIDTH
    return pl.pallas_call(
        _inproj_kernel,
        grid=(m // tm, n_col_blocks),
        in_specs=[
            pl.BlockSpec((tm, D_MODEL), lambda i, j: (i, 0)),
            pl.BlockSpec((D_MODEL, WIDTH), lambda i, j: (0, j)),
            pl.BlockSpec((1, WIDTH), lambda i, j: (0, jnp.maximum(j - COL_GATE, 0))),
            pl.BlockSpec((1, HEAD_DIM), lambda i, j: (0, 0)),
            pl.BlockSpec((1, HEAD_DIM), lambda i, j: (0, 0)),
        ],
        out_specs=pl.BlockSpec((tm, WIDTH), lambda i, j: (i, j)),
        out_shape=jax.ShapeDtypeStruct((m, IN_COLS), BF16),
        compiler_params=pltpu.CompilerParams(
            dimension_semantics=("parallel", "parallel"), vmem_limit_bytes=VMEM_LIMIT),
        name="inproj",
    )(hn, w_in, b_gate, q_norm, k_norm)


def _sb_kernel(q_ref, k_ref, v_ref, o_ref, *, blk):
    qi = pl.program_id(2)
    q = q_ref[...]
    row = lax.broadcasted_iota(jnp.int32, (blk, blk), 0)
    col = lax.broadcasted_iota(jnp.int32, (blk, blk), 1)
    later_sum = (row > col).astype(BF16)
    causal = col < row

    def sweep_block(kb, run, acc, diagonal):
        start = pl.multiple_of(kb * blk, blk)
        k = k_ref[pl.ds(start, blk), :]
        v = v_ref[pl.ds(start, blk), :]
        z = _dot_nt(q, k) * SCALE
        log_keep = jnp.minimum(-z, 0.0) - jnp.log(1.0 + jnp.exp(-jnp.abs(z)))
        log_beta = z + log_keep
        if diagonal:
            log_keep = jnp.where(causal, log_keep, 0.0)
        hi = log_keep.astype(BF16)
        lo = (log_keep - hi.astype(F32)).astype(BF16)
        later = _dot(hi, later_sum) + _dot(lo, later_sum)
        w = jnp.exp(log_beta + later + run)
        if diagonal:
            w = jnp.where(causal, w, 0.0)
        acc = acc + _dot(w.astype(BF16), v)
        run = run + jnp.sum(log_keep, axis=1, keepdims=True)
        return run, acc

    run0 = jnp.zeros((blk, 1), F32)
    acc0 = jnp.zeros((blk, HEAD_DIM), F32)
    run, acc = sweep_block(qi, run0, acc0, True)

    def cond(c):
        kb, run, _ = c
        return jnp.logical_and(kb >= 0, jnp.max(run) > LOG_F32_UNDERFLOW)

    def body(c):
        kb, run, acc = c
        run, acc = sweep_block(kb, run, acc, False)
        return kb - 1, run, acc

    _, _, acc = lax.while_loop(cond, body, (qi - 1, run, acc))
    o_ref[...] = acc.astype(BF16)


def _sb_attention(proj3, *, blk=256):
    b, s, _ = proj3.shape
    return pl.pallas_call(
        functools.partial(_sb_kernel, blk=blk),
        grid=(b, N_HEADS, s // blk),
        in_specs=[
            pl.BlockSpec((None, blk, HEAD_DIM), lambda bi, h, i: (bi, i, COL_Q_SB * N_HEADS + h)),
            pl.BlockSpec((None, s, HEAD_DIM), lambda bi, h, i: (bi, 0, COL_K_SB * N_HEADS + h)),
            pl.BlockSpec((None, s, HEAD_DIM), lambda bi, h, i: (bi, 0, COL_V_SB * N_HEADS + h)),
        ],
        out_specs=pl.BlockSpec((None, blk, HEAD_DIM), lambda bi, h, i: (bi, i, h)),
        out_shape=jax.ShapeDtypeStruct((b, s, WIDTH), BF16),
        compiler_params=pltpu.CompilerParams(
            dimension_semantics=("parallel", "parallel", "parallel"), vmem_limit_bytes=VMEM_LIMIT),
        name="sb_attn",
    )(proj3, proj3, proj3)


CA_TILE = N_PAST_CHUNKS * CHUNK


def _ca_kernel(q_ref, kp_ref, ko_ref, vp_ref, vo_ref, bias_ref, o_ref):
    i = pl.program_id(2)
    q = q_ref[...]
    s_own = _dot_nt(q, ko_ref[...]) * SCALE + bias_ref[1]
    s_prev = _dot_nt(q, kp_ref[...]) * SCALE + bias_ref[0]
    s_prev = jnp.where(i > 0, s_prev, MASK_BIAS)
    m = jnp.maximum(jnp.max(s_own, axis=1, keepdims=True), jnp.max(s_prev, axis=1, keepdims=True))
    p_own = jnp.exp(s_own - m)
    p_prev = jnp.exp(s_prev - m)
    denom = jnp.sum(p_own, axis=1, keepdims=True) + jnp.sum(p_prev, axis=1, keepdims=True)
    o = _dot(p_own.astype(BF16), vo_ref[...]) + _dot(p_prev.astype(BF16), vp_ref[...])
    o_ref[...] = (o / denom).astype(BF16)


def _ca_bias_tables(rel_bias):
    t = CA_TILE
    qpos = jnp.arange(t)[:, None]
    kpos = jnp.arange(t)[None, :]

    def table(dist, valid):
        idx = jnp.clip(dist, -(CHUNK - 1), REL_CLIP_PAST) + (CHUNK - 1)
        return jnp.where(valid[None], rel_bias[:, idx], MASK_BIAS)

    prev = table(qpos - kpos + t, kpos // CHUNK >= qpos // CHUNK)
    own = table(qpos - kpos, kpos // CHUNK <= qpos // CHUNK)
    return jnp.stack([prev, own], axis=1).astype(F32)


def _ca_attention(proj3, bias):
    b, s, _ = proj3.shape
    t = CA_TILE

    def col(c):
        return lambda h, bi, i: (bi, i, c * N_HEADS + h)

    def col_prev(c):
        return lambda h, bi, i: (bi, jnp.maximum(i - 1, 0), c * N_HEADS + h)

    blk = (None, t, HEAD_DIM)
    return pl.pallas_call(
        _ca_kernel,
        grid=(N_HEADS, b, s // t),
        in_specs=[
            pl.BlockSpec(blk, col(COL_Q_CA)),
            pl.BlockSpec(blk, col_prev(COL_K_CA)),
            pl.BlockSpec(blk, col(COL_K_CA)),
            pl.BlockSpec(blk, col_prev(COL_V_CA)),
            pl.BlockSpec(blk, col(COL_V_CA)),
            pl.BlockSpec((None, 2, t, t), lambda h, bi, i: (h, 0, 0, 0)),
        ],
        out_specs=pl.BlockSpec(blk, lambda h, bi, i: (bi, i, h)),
        out_shape=jax.ShapeDtypeStruct((b, s, WIDTH), BF16),
        compiler_params=pltpu.CompilerParams(
            dimension_semantics=("parallel", "parallel", "parallel"), vmem_limit_bytes=VMEM_LIMIT),
        name="ca_attn",
    )(proj3, proj3, proj3, proj3, proj3, bias)


def _merge_kernel(x_ref, a_ref, c_ref, ga_ref, gb_ref, woa_ref, wob_ref, wout_ref, o_ref):
    ya = _dot(a_ref[...], woa_ref[...])
    yb = _dot(c_ref[...], wob_ref[...])
    merged = ga_ref[...].astype(F32) * ya + gb_ref[...].astype(F32) * yb
    o_ref[...] = x_ref[...] + _dot(merged.astype(BF16), wout_ref[...])


def _merge(x1, att_sb, att_ca, proj, w_o_sb, w_o_ca, w_out, *, tm=512):
    m = x1.shape[0]
    gate_blk = COL_GATE * WIDTH // D_MODEL
    resident = dict(pipeline_mode=pl.Buffered(1))
    return pl.pallas_call(
        _merge_kernel,
        grid=(m // tm,),
        in_specs=[
            pl.BlockSpec((tm, D_MODEL), lambda i: (i, 0)),
            pl.BlockSpec((tm, WIDTH), lambda i: (i, 0)),
            pl.BlockSpec((tm, WIDTH), lambda i: (i, 0)),
            pl.BlockSpec((tm, D_MODEL), lambda i: (i, gate_blk)),
            pl.BlockSpec((tm, D_MODEL), lambda i: (i, gate_blk + 1)),
            pl.BlockSpec((WIDTH, D_MODEL), lambda i: (0, 0), **resident),
            pl.BlockSpec((WIDTH, D_MODEL), lambda i: (0, 0), **resident),
            pl.BlockSpec((D_MODEL, D_MODEL), lambda i: (0, 0), **resident),
        ],
        out_specs=pl.BlockSpec((tm, D_MODEL), lambda i: (i, 0)),
        out_shape=jax.ShapeDtypeStruct((m, D_MODEL), F32),
        compiler_params=pltpu.CompilerParams(
            dimension_semantics=("parallel",), vmem_limit_bytes=VMEM_LIMIT),
        name="merge",
    )(x1, att_sb, att_ca, proj, proj, w_o_sb, w_o_ca, w_out)


def kernel(x, ffn1_norm, ffn1_w_gate, ffn1_w_up, ffn1_w_down, mix_norm, w_in, b_gate,
           q_norm_ca, k_norm_ca, rel_bias, w_o_sb, w_o_ca, w_out, ffn2_norm,
           ffn2_w_gate, ffn2_w_up, ffn2_w_down, final_norm):
    b, s, d = x.shape
    depth = ffn1_norm.shape[0]
    bf = lambda w: w.astype(BF16)
    row = lambda v: v.reshape(1, -1)
    xf = x.reshape(b * s, d)
    for l in range(depth):
        x1, hn = _ffn(xf, row(ffn1_norm[l]), bf(ffn1_w_gate[l]), bf(ffn1_w_up[l]), bf(ffn1_w_down[l]),
                      row(mix_norm[l]), emit_residual=True)
        proj = _inproj(hn, bf(w_in[l]), row(b_gate[l]), row(q_norm_ca[l]), row(k_norm_ca[l]))
        proj3 = proj.reshape(b, s, IN_COLS)
        att_sb = _sb_attention(proj3).reshape(b * s, WIDTH)
        att_ca = _ca_attention(proj3, _ca_bias_tables(rel_bias[l])).reshape(b * s, WIDTH)
        x2 = _merge(x1, att_sb, att_ca, proj, bf(w_o_sb[l]), bf(w_o_ca[l]), bf(w_out[l]))
        (xf,) = _ffn(x2, row(ffn2_norm[l]), bf(ffn2_w_gate[l]), bf(ffn2_w_up[l]), bf(ffn2_w_down[l]),
                     row(final_norm[l]), emit_residual=False)
    return xf.reshape(b, s, d)
```

```python
import functools

import jax
import jax.numpy as jnp
from jax import lax
from jax.experimental import pallas as pl
from jax.experimental.pallas import tpu as pltpu

D_MODEL = 2048
D_FF = 5632
HEAD_DIM = 128
N_HEADS = 8
WIDTH = N_HEADS * HEAD_DIM
CHUNK = 64
N_PAST_CHUNKS = 8
REL_CLIP_PAST = 128
N_REL = REL_CLIP_PAST + CHUNK
N_BRANCH = 2
IN_COLS = 6 * WIDTH + N_BRANCH * D_MODEL
EPS = 1e-6
SCALE = HEAD_DIM ** -0.5

COL_Q_SB, COL_K_SB, COL_V_SB, COL_Q_CA, COL_K_CA, COL_V_CA, COL_GATE = range(7)

V7X_VMEM_BYTES = 64 * 1024 * 1024
VMEM_LIMIT = V7X_VMEM_BYTES - 8 * 1024 * 1024

LOG_F32_UNDERFLOW = -104.0
LOG2_E = 1.4426950408889634
MASK_BIAS = -1e30

F32 = jnp.float32
BF16 = jnp.bfloat16


def _dot(a, b):
    return jnp.dot(a, b, preferred_element_type=F32)


def _dot_nt(a, b):
    return lax.dot_general(a, b, (((1,), (1,)), ((), ())), preferred_element_type=F32)


def _rmsnorm_f32(x, gain):
    return x * lax.rsqrt(jnp.mean(x * x, axis=-1, keepdims=True) + EPS) * gain


def _ffn_kernel(x_ref, nin_ref, wg_ref, wu_ref, wd_ref, nout_ref, *rest, emit_residual):
    if emit_residual:
        o_ref, hn_out_ref, hn_scr = rest
    else:
        o_ref, hn_scr = rest
    f = pl.program_id(1)

    @pl.when(f == 0)
    def _():
        x = x_ref[...]
        hn_scr[...] = _rmsnorm_f32(x, nin_ref[...]).astype(BF16)
        o_ref[...] = x

    h = hn_scr[...]
    g = _dot(h, wg_ref[...])
    u = _dot(h, wu_ref[...])
    a = (g * u * 0.5) / (1.0 + jnp.exp(-g))
    o_ref[...] += _dot(a.astype(BF16), wd_ref[...])

    @pl.when(f == pl.num_programs(1) - 1)
    def _():
        y = _rmsnorm_f32(o_ref[...], nout_ref[...])
        if emit_residual:
            hn_out_ref[...] = y.astype(BF16)
        else:
            o_ref[...] = y


def _ffn(x, norm_in, wg, wu, wd, norm_out, *, emit_residual, tm=512, tf=512):
    m = x.shape[0]
    grid = (m // tm, D_FF // tf)
    out_shape = [jax.ShapeDtypeStruct((m, D_MODEL), F32)]
    out_specs = [pl.BlockSpec((tm, D_MODEL), lambda i, f: (i, 0))]
    if emit_residual:
        out_shape.append(jax.ShapeDtypeStruct((m, D_MODEL), BF16))
        out_specs.append(pl.BlockSpec((tm, D_MODEL), lambda i, f: (i, 0)))
    res = pl.pallas_call(
        functools.partial(_ffn_kernel, emit_residual=emit_residual),
        grid=grid,
        in_specs=[
            pl.BlockSpec((tm, D_MODEL), lambda i, f: (i, 0)),
            pl.BlockSpec((1, D_MODEL), lambda i, f: (0, 0)),
            pl.BlockSpec((D_MODEL, tf), lambda i, f: (0, f)),
            pl.BlockSpec((D_MODEL, tf), lambda i, f: (0, f)),
            pl.BlockSpec((tf, D_MODEL), lambda i, f: (f, 0)),
            pl.BlockSpec((1, D_MODEL), lambda i, f: (0, 0)),
        ],
        out_specs=out_specs,
        out_shape=out_shape,
        scratch_shapes=[pltpu.VMEM((tm, D_MODEL), BF16)],
        compiler_params=pltpu.CompilerParams(
            dimension_semantics=("parallel", "arbitrary"), vmem_limit_bytes=VMEM_LIMIT),
        name="ffn_res" if emit_residual else "ffn_final",
    )(x, norm_in, wg, wu, wd, norm_out)
    return res


def _inproj_kernel(h_ref, w_ref, bg_ref, qn_ref, kn_ref, o_ref):
    j = pl.program_id(1)
    r = _dot(h_ref[...], w_ref[...])

    @pl.when((j <= COL_V_SB) | (j == COL_V_CA))
    def _():
        o_ref[...] = r.astype(BF16)

    def head_norm(gain_ref):
        for hd in range(N_HEADS):
            sl = slice(hd * HEAD_DIM, (hd + 1) * HEAD_DIM)
            o_ref[:, sl] = _rmsnorm_f32(r[:, sl], gain_ref[...]).astype(BF16)

    @pl.when(j == COL_Q_CA)
    def _():
        head_norm(qn_ref)

    @pl.when(j == COL_K_CA)
    def _():
        head_norm(kn_ref)

    @pl.when(j >= COL_GATE)
    def _():
        o_ref[...] = (1.0 / (1.0 + jnp.exp(-(r + bg_ref[...])))).astype(BF16)


def _inproj(hn, w_in, b_gate, q_norm, k_norm, *, tm=1024):
    m = hn.shape[0]
    n_col_blocks = IN_COLS // WIDTH
    return pl.pallas_call(
        _inproj_kernel,
        grid=(m // tm, n_col_blocks),
        in_specs=[
            pl.BlockSpec((tm, D_MODEL), lambda i, j: (i, 0)),
            pl.BlockSpec((D_MODEL, WIDTH), lambda i, j: (0, j)),
            pl.BlockSpec((1, WIDTH), lambda i, j: (0, jnp.maximum(j - COL_GATE, 0))),
            pl.BlockSpec((1, HEAD_DIM), lambda i, j: (0, 0)),
            pl.BlockSpec((1, HEAD_DIM), lambda i, j: (0, 0)),
        ],
        out_specs=pl.BlockSpec((tm, WIDTH), lambda i, j: (i, j)),
        out_shape=jax.ShapeDtypeStruct((m, IN_COLS), BF16),
        compiler_params=pltpu.CompilerParams(
            dimension_semantics=("parallel", "parallel"), vmem_limit_bytes=VMEM_LIMIT),
        name="inproj",
    )(hn, w_in, b_gate, q_norm, k_norm)


SB_BLK = 256
SB_HEADS = 4


def _sb_kernel(q_ref, k_ref, v_ref, o_ref):
    blk = SB_BLK
    row = lax.broadcasted_iota(jnp.int32, (blk, blk), 0)
    col = lax.broadcasted_iota(jnp.int32, (blk, blk), 1)
    later_sum = (row > col).astype(BF16)
    causal = col < row

    def sweep_block(qs, kb, state, diagonal):
        start = pl.multiple_of(kb * blk, blk)
        log_keeps, log_betas, splits = [], [], []
        for hd in range(SB_HEADS):
            lanes = slice(hd * HEAD_DIM, (hd + 1) * HEAD_DIM)
            nz = _dot_nt(qs[hd], k_ref[pl.ds(start, blk), lanes]) * (-SCALE * LOG2_E)
            log_keep = jnp.minimum(nz, 0.0) - jnp.log2(1.0 + jnp.exp2(-jnp.abs(nz)))
            log_betas.append(log_keep - nz)
            if diagonal:
                log_keep = jnp.where(causal, log_keep, 0.0)
            log_keeps.append(log_keep)
            hi = log_keep.astype(BF16)
            splits += [hi, (log_keep - hi.astype(F32)).astype(BF16)]
        later_all = _dot(jnp.concatenate(splits, axis=0), later_sum)
        new_state = []
        for hd in range(SB_HEADS):
            run, acc = state[hd]
            lanes = slice(hd * HEAD_DIM, (hd + 1) * HEAD_DIM)
            later = later_all[2 * hd * blk:(2 * hd + 1) * blk] + later_all[(2 * hd + 1) * blk:(2 * hd + 2) * blk]
            w = jnp.exp2(log_betas[hd] + later + run)
            if diagonal:
                w = jnp.where(causal, w, 0.0)
            acc = acc + _dot(w.astype(BF16), v_ref[pl.ds(start, blk), lanes])
            run = run + jnp.sum(log_keeps[hd], axis=1, keepdims=True)
            new_state.append((run, acc))
        return tuple(new_state)

    def query_tile(qi, carry):
        q0 = pl.multiple_of(qi * blk, blk)
        qs = [q_ref[pl.ds(q0, blk), hd * HEAD_DIM:(hd + 1) * HEAD_DIM] for hd in range(SB_HEADS)]
        run0 = jnp.zeros((blk, 1), F32)
        acc0 = jnp.zeros((blk, HEAD_DIM), F32)
        state = sweep_block(qs, qi, ((run0, acc0),) * SB_HEADS, True)

        def cond(c):
            kb, st = c
            worst = functools.reduce(jnp.maximum, [run for run, _ in st])
            return jnp.logical_and(kb >= 0, jnp.max(worst) > LOG_F32_UNDERFLOW * LOG2_E)

        def body(c):
            kb, st = c
            return kb - 1, sweep_block(qs, kb, st, False)

        _, state = lax.while_loop(cond, body, (qi - 1, state))
        for hd in range(SB_HEADS):
            o_ref[pl.ds(q0, blk), hd * HEAD_DIM:(hd + 1) * HEAD_DIM] = state[hd][1].astype(BF16)
        return carry

    lax.fori_loop(0, q_ref.shape[0] // blk, query_tile, 0)


def _sb_attention(proj3):
    b, s, _ = proj3.shape
    groups = N_HEADS // SB_HEADS
    blk = (None, s, SB_HEADS * HEAD_DIM)
    return pl.pallas_call(
        _sb_kernel,
        grid=(b, groups),
        in_specs=[
            pl.BlockSpec(blk, lambda bi, g: (bi, 0, COL_Q_SB * groups + g)),
            pl.BlockSpec(blk, lambda bi, g: (bi, 0, COL_K_SB * groups + g)),
            pl.BlockSpec(blk, lambda bi, g: (bi, 0, COL_V_SB * groups + g)),
        ],
        out_specs=pl.BlockSpec(blk, lambda bi, g: (bi, 0, g)),
        out_shape=jax.ShapeDtypeStruct((b, s, WIDTH), BF16),
        compiler_params=pltpu.CompilerParams(
            dimension_semantics=("parallel", "parallel"), vmem_limit_bytes=VMEM_LIMIT),
        name="sb_attn",
    )(proj3, proj3, proj3)


CA_TQ = 256
CA_KBLKS = 1 + N_PAST_CHUNKS * CHUNK // CA_TQ
CA_WIN = CA_KBLKS * CA_TQ
CA_ROLL = 1024


def _ca_bias_row(rel_bias):
    far = N_PAST_CHUNKS * CHUNK - REL_CLIP_PAST + 1
    past = jnp.broadcast_to(rel_bias[:, N_REL - 1:], (N_HEADS, far))
    ramp = rel_bias[:, N_REL - 2::-1]
    future = jnp.broadcast_to(rel_bias[:, :1], (N_HEADS, CA_WIN + 1 - far - (N_REL - 1)))
    wrap = jnp.broadcast_to(rel_bias[:, N_REL - 1:], (N_HEADS, CA_ROLL - CA_WIN - 1))
    return jnp.concatenate([past, ramp, future, wrap], axis=1).astype(F32)[:, None, :]


def _ca_kernel(q_ref, k_ref, v_ref, brow_ref, o_ref, bias_scr):
    @pl.when(pl.program_id(1) == 0)
    def _():
        rows = jnp.broadcast_to(brow_ref[...], (CA_TQ, CA_ROLL))
        table = pltpu.roll(rows, 0, 1, stride=1, stride_axis=0)[:, :CA_WIN]
        q_chunk = lax.broadcasted_iota(jnp.int32, (CA_TQ, CA_WIN), 0) // CHUNK
        k_chunk = lax.broadcasted_iota(jnp.int32, (CA_TQ, CA_WIN), 1) // CHUNK
        in_band = (k_chunk >= q_chunk) & (k_chunk <= q_chunk + N_PAST_CHUNKS)
        bias_scr[...] = jnp.where(in_band, table, MASK_BIAS)

    def query_tile(t, carry):
        q0 = pl.multiple_of(t * CA_TQ, CA_TQ)
        q = q_ref[pl.ds(q0, CA_TQ), :]
        scores, starts = [], []
        for r in range(CA_KBLKS):
            kb = t - (CA_KBLKS - 1) + r
            k0 = pl.multiple_of(jnp.maximum(kb, 0) * CA_TQ, CA_TQ)
            s = _dot_nt(q, k_ref[pl.ds(k0, CA_TQ), :]) * SCALE + bias_scr[:, r * CA_TQ:(r + 1) * CA_TQ]
            if r < CA_KBLKS - 1:
                s = jnp.where(kb >= 0, s, MASK_BIAS)
            scores.append(s)
            starts.append(k0)
        m = functools.reduce(jnp.maximum, [jnp.max(s, axis=1, keepdims=True) for s in scores])
        ps = [jnp.exp(s - m) for s in scores]
        denom = functools.reduce(jnp.add, [jnp.sum(p, axis=1, keepdims=True) for p in ps])
        o = functools.reduce(
            jnp.add, [_dot(p.astype(BF16), v_ref[pl.ds(k0, CA_TQ), :]) for p, k0 in zip(ps, starts)])
        o_ref[pl.ds(q0, CA_TQ), :] = (o / denom).astype(BF16)
        return carry

    lax.fori_loop(0, q_ref.shape[0] // CA_TQ, query_tile, 0, unroll=2)


def _ca_attention(proj3, bias_row):
    b, s, _ = proj3.shape
    blk = (None, s, HEAD_DIM)

    def col(c):
        return lambda h, bi: (bi, 0, c * N_HEADS + h)

    return pl.pallas_call(
        _ca_kernel,
        grid=(N_HEADS, b),
        in_specs=[
            pl.BlockSpec(blk, col(COL_Q_CA)),
            pl.BlockSpec(blk, col(COL_K_CA)),
            pl.BlockSpec(blk, col(COL_V_CA)),
            pl.BlockSpec((None, 1, CA_ROLL), lambda h, bi: (h, 0, 0)),
        ],
        out_specs=pl.BlockSpec(blk, lambda h, bi: (bi, 0, h)),
        out_shape=jax.ShapeDtypeStruct((b, s, WIDTH), BF16),
        scratch_shapes=[pltpu.VMEM((CA_TQ, CA_WIN), F32)],
        compiler_params=pltpu.CompilerParams(
            dimension_semantics=("arbitrary", "arbitrary"), vmem_limit_bytes=VMEM_LIMIT),
        name="ca_attn",
    )(proj3, proj3, proj3, bias_row)


def _merge_kernel(x_ref, a_ref, c_ref, ga_ref, gb_ref, woa_ref, wob_ref, wout_ref, o_ref):
    ya = _dot(a_ref[...], woa_ref[...])
    yb = _dot(c_ref[...], wob_ref[...])
    merged = ga_ref[...].astype(F32) * ya + gb_ref[...].astype(F32) * yb
    o_ref[...] = x_ref[...] + _dot(merged.astype(BF16), wout_ref[...])


def _merge(x1, att_sb, att_ca, proj, w_o_sb, w_o_ca, w_out, *, tm=512):
    m = x1.shape[0]
    gate_blk = COL_GATE * WIDTH // D_MODEL
    resident = dict(pipeline_mode=pl.Buffered(1))
    return pl.pallas_call(
        _merge_kernel,
        grid=(m // tm,),
        in_specs=[
            pl.BlockSpec((tm, D_MODEL), lambda i: (i, 0)),
            pl.BlockSpec((tm, WIDTH), lambda i: (i, 0)),
            pl.BlockSpec((tm, WIDTH), lambda i: (i, 0)),
            pl.BlockSpec((tm, D_MODEL), lambda i: (i, gate_blk)),
            pl.BlockSpec((tm, D_MODEL), lambda i: (i, gate_blk + 1)),
            pl.BlockSpec((WIDTH, D_MODEL), lambda i: (0, 0), **resident),
            pl.BlockSpec((WIDTH, D_MODEL), lambda i: (0, 0), **resident),
            pl.BlockSpec((D_MODEL, D_MODEL), lambda i: (0, 0), **resident),
        ],
        out_specs=pl.BlockSpec((tm, D_MODEL), lambda i: (i, 0)),
        out_shape=jax.ShapeDtypeStruct((m, D_MODEL), F32),
        compiler_params=pltpu.CompilerParams(
            dimension_semantics=("parallel",), vmem_limit_bytes=VMEM_LIMIT),
        name="merge",
    )(x1, att_sb, att_ca, proj, proj, w_o_sb, w_o_ca, w_out)


def kernel(x, ffn1_norm, ffn1_w_gate, ffn1_w_up, ffn1_w_down, mix_norm, w_in, b_gate,
           q_norm_ca, k_norm_ca, rel_bias, w_o_sb, w_o_ca, w_out, ffn2_norm,
           ffn2_w_gate, ffn2_w_up, ffn2_w_down, final_norm):
    b, s, d = x.shape
    depth = ffn1_norm.shape[0]
    bf = lambda w: w.astype(BF16)
    row = lambda v: v.reshape(1, -1)
    xf = x.reshape(b * s, d)
    for l in range(depth):
        x1, hn = _ffn(xf, row(ffn1_norm[l]), bf(ffn1_w_gate[l]), bf(ffn1_w_up[l]), bf(ffn1_w_down[l]),
                      row(mix_norm[l]), emit_residual=True)
        proj = _inproj(hn, bf(w_in[l]), row(b_gate[l]), row(q_norm_ca[l]), row(k_norm_ca[l]))
        proj3 = proj.reshape(b, s, IN_COLS)
        att_sb = _sb_attention(proj3).reshape(b * s, WIDTH)
        att_ca = _ca_attention(proj3, _ca_bias_row(rel_bias[l])).reshape(b * s, WIDTH)
        x2 = _merge(x1, att_sb, att_ca, proj, bf(w_o_sb[l]), bf(w_o_ca[l]), bf(w_out[l]))
        (xf,) = _ffn(x2, row(ffn2_norm[l]), bf(ffn2_w_gate[l]), bf(ffn2_w_up[l]), bf(ffn2_w_down[l]),
                     row(final_norm[l]), emit_residual=False)
    return xf.reshape(b, s, d)
```

```python
import functools

import jax
import jax.numpy as jnp
from jax import lax
from jax.experimental import pallas as pl
from jax.experimental.pallas import tpu as pltpu

D_MODEL = 2048
D_FF = 5632
HEAD_DIM = 128
N_HEADS = 8
WIDTH = N_HEADS * HEAD_DIM
CHUNK = 64
N_PAST_CHUNKS = 8
REL_CLIP_PAST = 128
N_REL = REL_CLIP_PAST + CHUNK
N_BRANCH = 2
IN_COLS = 6 * WIDTH + N_BRANCH * D_MODEL
EPS = 1e-6
SCALE = HEAD_DIM ** -0.5

COL_Q_SB, COL_K_SB, COL_V_SB, COL_Q_CA, COL_K_CA, COL_V_CA, COL_GATE = range(7)

V7X_VMEM_BYTES = 64 * 1024 * 1024
VMEM_LIMIT = V7X_VMEM_BYTES - 8 * 1024 * 1024

LOG_F32_UNDERFLOW = -104.0
LOG2_E = 1.4426950408889634
MASK_BIAS = -1e30

F32 = jnp.float32
BF16 = jnp.bfloat16


def _dot(a, b):
    return jnp.dot(a, b, preferred_element_type=F32)


def _dot_nt(a, b):
    return lax.dot_general(a, b, (((1,), (1,)), ((), ())), preferred_element_type=F32)


def _rmsnorm_f32(x, gain):
    return x * lax.rsqrt(jnp.mean(x * x, axis=-1, keepdims=True) + EPS) * gain


def _ffn_kernel(x_ref, nin_ref, wgu_ref, wd_ref, nout_ref, *rest, emit_residual):
    if emit_residual:
        o_ref, hn_out_ref, hn_scr = rest
    else:
        o_ref, hn_scr = rest
    f = pl.program_id(1)
    tf = wd_ref.shape[0]

    @pl.when(f == 0)
    def _():
        x = x_ref[...]
        hn_scr[...] = _rmsnorm_f32(x, nin_ref[...]).astype(BF16)
        o_ref[...] = x

    gu = _dot(hn_scr[...], wgu_ref[...])
    g = gu[:, :tf]
    u = gu[:, tf:]
    a = (g * u * 0.5) / (1.0 + jnp.exp(-g))
    o_ref[...] += _dot(a.astype(BF16), wd_ref[...])

    @pl.when(f == pl.num_programs(1) - 1)
    def _():
        y = _rmsnorm_f32(o_ref[...], nout_ref[...])
        if emit_residual:
            hn_out_ref[...] = y.astype(BF16)
        else:
            o_ref[...] = y


def _interleave_gate_up(wg, wu, tf):
    d, ff = wg.shape
    both = jnp.stack([wg.reshape(d, ff // tf, tf), wu.reshape(d, ff // tf, tf)], axis=2)
    return both.astype(BF16).reshape(d, 2 * ff)


def _ffn(x, norm_in, wg, wu, wd, norm_out, *, emit_residual, tm=512, tf=512):
    m = x.shape[0]
    grid = (m // tm, D_FF // tf)
    wgu = _interleave_gate_up(wg, wu, tf)
    wd = wd.astype(BF16)
    out_shape = [jax.ShapeDtypeStruct((m, D_MODEL), F32)]
    out_specs = [pl.BlockSpec((tm, D_MODEL), lambda i, f: (i, 0))]
    if emit_residual:
        out_shape.append(jax.ShapeDtypeStruct((m, D_MODEL), BF16))
        out_specs.append(pl.BlockSpec((tm, D_MODEL), lambda i, f: (i, 0)))
    res = pl.pallas_call(
        functools.partial(_ffn_kernel, emit_residual=emit_residual),
        grid=grid,
        in_specs=[
            pl.BlockSpec((tm, D_MODEL), lambda i, f: (i, 0)),
            pl.BlockSpec((1, D_MODEL), lambda i, f: (0, 0)),
            pl.BlockSpec((D_MODEL, 2 * tf), lambda i, f: (0, f)),
            pl.BlockSpec((tf, D_MODEL), lambda i, f: (f, 0)),
            pl.BlockSpec((1, D_MODEL), lambda i, f: (0, 0)),
        ],
        out_specs=out_specs,
        out_shape=out_shape,
        scratch_shapes=[pltpu.VMEM((tm, D_MODEL), BF16)],
        compiler_params=pltpu.CompilerParams(
            dimension_semantics=("parallel", "arbitrary"), vmem_limit_bytes=VMEM_LIMIT),
        name="ffn_res" if emit_residual else "ffn_final",
    )(x, norm_in, wgu, wd, norm_out)
    return res


def _inproj_kernel(h_ref, w_ref, *rest, epilogue):
    if epilogue == "plain":
        o_ref, w_scr = rest
    else:
        aux_ref, o_ref, w_scr = rest

    @pl.when(pl.program_id(1) == 0)
    def _():
        w_scr[...] = w_ref[...].astype(BF16)

    r = _dot(h_ref[...], w_scr[...])
    if epilogue == "plain":
        o_ref[...] = r.astype(BF16)
    elif epilogue == "head_norm":
        gains = aux_ref[pl.ds(pl.program_id(0), 1), :]
        for hd in range(N_HEADS):
            sl = slice(hd * HEAD_DIM, (hd + 1) * HEAD_DIM)
            o_ref[:, sl] = _rmsnorm_f32(r[:, sl], gains).astype(BF16)
    else:
        assert epilogue == "gate"
        o_ref[...] = (1.0 / (1.0 + jnp.exp(-(r + aux_ref[...])))).astype(BF16)


def _inproj(hn, w_in, w_col_blocks, epilogue, aux=None, *, tm=1024):
    m = hn.shape[0]
    n = len(w_col_blocks)
    first, last = w_col_blocks[0], w_col_blocks[-1]
    assert list(w_col_blocks[:-1]) == list(range(first, first + n - 1)) and last >= first + n - 1
    assert (aux is None) == (epilogue == "plain")
    aux_specs = {
        "plain": [],
        "gate": [pl.BlockSpec((1, WIDTH), lambda j, i: (0, j))],
        "head_norm": [pl.BlockSpec((n, HEAD_DIM), lambda j, i: (0, 0))],
    }[epilogue]
    return pl.pallas_call(
        functools.partial(_inproj_kernel, epilogue=epilogue),
        grid=(n, m // tm),
        in_specs=[
            pl.BlockSpec((tm, D_MODEL), lambda j, i: (i, 0)),
            pl.BlockSpec((D_MODEL, WIDTH), lambda j, i: (0, first + j + (last - first - n + 1) * (j // (n - 1)))),
            *aux_specs,
        ],
        out_specs=pl.BlockSpec((tm, WIDTH), lambda j, i: (i, j)),
        out_shape=jax.ShapeDtypeStruct((m, n * WIDTH), BF16),
        scratch_shapes=[pltpu.VMEM((D_MODEL, WIDTH), BF16)],
        compiler_params=pltpu.CompilerParams(
            dimension_semantics=("arbitrary", "arbitrary"), vmem_limit_bytes=VMEM_LIMIT),
        name="inproj_" + epilogue,
    )(hn, w_in, *([] if aux is None else [aux]))


SB_BLK = 256
SB_HEADS = 4


def _sb_kernel(q_ref, k_ref, v_ref, o_ref):
    blk = SB_BLK
    row = lax.broadcasted_iota(jnp.int32, (blk, blk), 0)
    col = lax.broadcasted_iota(jnp.int32, (blk, blk), 1)
    later_sum = (row > col).astype(BF16)
    causal = col < row

    def sweep_block(qs, kb, state, diagonal):
        start = pl.multiple_of(kb * blk, blk)
        log_keeps, log_betas, splits = [], [], []
        for hd in range(SB_HEADS):
            lanes = slice(hd * HEAD_DIM, (hd + 1) * HEAD_DIM)
            nz = _dot_nt(qs[hd], k_ref[pl.ds(start, blk), lanes]) * (-SCALE * LOG2_E)
            log_keep = jnp.minimum(nz, 0.0) - jnp.log2(1.0 + jnp.exp2(-jnp.abs(nz)))
            log_betas.append(log_keep - nz)
            if diagonal:
                log_keep = jnp.where(causal, log_keep, 0.0)
            log_keeps.append(log_keep)
            hi = log_keep.astype(BF16)
            splits += [hi, (log_keep - hi.astype(F32)).astype(BF16)]
        later_all = _dot(jnp.concatenate(splits, axis=0), later_sum)
        new_state = []
        for hd in range(SB_HEADS):
            run, acc = state[hd]
            lanes = slice(hd * HEAD_DIM, (hd + 1) * HEAD_DIM)
            later = later_all[2 * hd * blk:(2 * hd + 1) * blk] + later_all[(2 * hd + 1) * blk:(2 * hd + 2) * blk]
            w = jnp.exp2(log_betas[hd] + later + run)
            if diagonal:
                w = jnp.where(causal, w, 0.0)
            acc = acc + _dot(w.astype(BF16), v_ref[pl.ds(start, blk), lanes])
            run = run + jnp.sum(log_keeps[hd], axis=1, keepdims=True)
            new_state.append((run, acc))
        return tuple(new_state)

    def query_tile(qi, carry):
        q0 = pl.multiple_of(qi * blk, blk)
        qs = [q_ref[pl.ds(q0, blk), hd * HEAD_DIM:(hd + 1) * HEAD_DIM] for hd in range(SB_HEADS)]
        run0 = jnp.zeros((blk, 1), F32)
        acc0 = jnp.zeros((blk, HEAD_DIM), F32)
        state = sweep_block(qs, qi, ((run0, acc0),) * SB_HEADS, True)

        def cond(c):
            kb, st = c
            worst = functools.reduce(jnp.maximum, [run for run, _ in st])
            return jnp.logical_and(kb >= 0, jnp.max(worst) > LOG_F32_UNDERFLOW * LOG2_E)

        def body(c):
            kb, st = c
            return kb - 1, sweep_block(qs, kb, st, False)

        _, state = lax.while_loop(cond, body, (qi - 1, state))
        for hd in range(SB_HEADS):
            o_ref[pl.ds(q0, blk), hd * HEAD_DIM:(hd + 1) * HEAD_DIM] = state[hd][1].astype(BF16)
        return carry

    lax.fori_loop(0, q_ref.shape[0] // blk, query_tile, 0)


def _sb_attention(qkv, q_col, k_col, v_col):
    b, s, _ = qkv.shape
    groups = N_HEADS // SB_HEADS
    blk = (None, s, SB_HEADS * HEAD_DIM)

    def col(c):
        return lambda bi, g: (bi, 0, c * groups + g)

    return pl.pallas_call(
        _sb_kernel,
        grid=(b, groups),
        in_specs=[pl.BlockSpec(blk, col(q_col)), pl.BlockSpec(blk, col(k_col)), pl.BlockSpec(blk, col(v_col))],
        out_specs=pl.BlockSpec(blk, col(0)),
        out_shape=jax.ShapeDtypeStruct((b, s, WIDTH), BF16),
        compiler_params=pltpu.CompilerParams(
            dimension_semantics=("parallel", "parallel"), vmem_limit_bytes=VMEM_LIMIT),
        name="sb_attn",
    )(qkv, qkv, qkv)


CA_TQ = 256
CA_KBLKS = 1 + N_PAST_CHUNKS * CHUNK // CA_TQ
CA_WIN = CA_KBLKS * CA_TQ
CA_ROLL = 1024


def _ca_bias_row(rel_bias):
    far = N_PAST_CHUNKS * CHUNK - REL_CLIP_PAST + 1
    past = jnp.broadcast_to(rel_bias[:, N_REL - 1:], (N_HEADS, far))
    ramp = rel_bias[:, N_REL - 2::-1]
    future = jnp.broadcast_to(rel_bias[:, :1], (N_HEADS, CA_WIN + 1 - far - (N_REL - 1)))
    wrap = jnp.broadcast_to(rel_bias[:, N_REL - 1:], (N_HEADS, CA_ROLL - CA_WIN - 1))
    return jnp.concatenate([past, ramp, future, wrap], axis=1).astype(F32)[:, None, :]


def _ca_kernel(q_ref, k_ref, v_ref, brow_ref, o_ref, bias_scr):
    @pl.when(pl.program_id(1) == 0)
    def _():
        rows = jnp.broadcast_to(brow_ref[...], (CA_TQ, CA_ROLL))
        table = pltpu.roll(rows, 0, 1, stride=1, stride_axis=0)[:, :CA_WIN]
        q_chunk = lax.broadcasted_iota(jnp.int32, (CA_TQ, CA_WIN), 0) // CHUNK
        k_chunk = lax.broadcasted_iota(jnp.int32, (CA_TQ, CA_WIN), 1) // CHUNK
        in_band = (k_chunk >= q_chunk) & (k_chunk <= q_chunk + N_PAST_CHUNKS)
        bias_scr[...] = jnp.where(in_band, table, MASK_BIAS)

    def query_tile(t, carry):
        q0 = pl.multiple_of(t * CA_TQ, CA_TQ)
        q = q_ref[pl.ds(q0, CA_TQ), :]
        scores, starts = [], []
        for r in range(CA_KBLKS):
            kb = t - (CA_KBLKS - 1) + r
            k0 = pl.multiple_of(jnp.maximum(kb, 0) * CA_TQ, CA_TQ)
            s = _dot_nt(q, k_ref[pl.ds(k0, CA_TQ), :]) * SCALE + bias_scr[:, r * CA_TQ:(r + 1) * CA_TQ]
            if r < CA_KBLKS - 1:
                s = jnp.where(kb >= 0, s, MASK_BIAS)
            scores.append(s)
            starts.append(k0)
        m = functools.reduce(jnp.maximum, [jnp.max(s, axis=1, keepdims=True) for s in scores])
        ps = [jnp.exp(s - m) for s in scores]
        denom = functools.reduce(jnp.add, [jnp.sum(p, axis=1, keepdims=True) for p in ps])
        o = functools.reduce(
            jnp.add, [_dot(p.astype(BF16), v_ref[pl.ds(k0, CA_TQ), :]) for p, k0 in zip(ps, starts)])
        o_ref[pl.ds(q0, CA_TQ), :] = (o / denom).astype(BF16)
        return carry

    lax.fori_loop(0, q_ref.shape[0] // CA_TQ, query_tile, 0, unroll=2)


def _ca_attention(qk, v, v_col, bias_row):
    b, s, _ = qk.shape
    blk = (None, s, HEAD_DIM)

    def col(c):
        return lambda h, bi: (bi, 0, c * N_HEADS + h)

    return pl.pallas_call(
        _ca_kernel,
        grid=(N_HEADS, b),
        in_specs=[
            pl.BlockSpec(blk, col(0)),
            pl.BlockSpec(blk, col(1)),
            pl.BlockSpec(blk, col(v_col)),
            pl.BlockSpec((None, 1, CA_ROLL), lambda h, bi: (h, 0, 0)),
        ],
        out_specs=pl.BlockSpec(blk, lambda h, bi: (bi, 0, h)),
        out_shape=jax.ShapeDtypeStruct((b, s, WIDTH), BF16),
        scratch_shapes=[pltpu.VMEM((CA_TQ, CA_WIN), F32)],
        compiler_params=pltpu.CompilerParams(
            dimension_semantics=("arbitrary", "arbitrary"), vmem_limit_bytes=VMEM_LIMIT),
        name="ca_attn",
    )(qk, qk, v, bias_row)


def _merge_kernel(x_ref, a_ref, c_ref, ga_ref, gb_ref, woa_ref, wob_ref, wout_ref, o_ref):
    ya = _dot(a_ref[...], woa_ref[...])
    yb = _dot(c_ref[...], wob_ref[...])
    merged = ga_ref[...].astype(F32) * ya + gb_ref[...].astype(F32) * yb
    o_ref[...] = x_ref[...] + _dot(merged.astype(BF16), wout_ref[...])


def _merge(x1, att_sb, att_ca, gates, w_o_sb, w_o_ca, w_out, *, tm=512):
    m = x1.shape[0]
    resident = dict(pipeline_mode=pl.Buffered(1))
    return pl.pallas_call(
        _merge_kernel,
        grid=(m // tm,),
        in_specs=[
            pl.BlockSpec((tm, D_MODEL), lambda i: (i, 0)),
            pl.BlockSpec((tm, WIDTH), lambda i: (i, 0)),
            pl.BlockSpec((tm, WIDTH), lambda i: (i, 0)),
            pl.BlockSpec((tm, D_MODEL), lambda i: (i, 0)),
            pl.BlockSpec((tm, D_MODEL), lambda i: (i, 1)),
            pl.BlockSpec((WIDTH, D_MODEL), lambda i: (0, 0), **resident),
            pl.BlockSpec((WIDTH, D_MODEL), lambda i: (0, 0), **resident),
            pl.BlockSpec((D_MODEL, D_MODEL), lambda i: (0, 0), **resident),
        ],
        out_specs=pl.BlockSpec((tm, D_MODEL), lambda i: (i, 0)),
        out_shape=jax.ShapeDtypeStruct((m, D_MODEL), F32),
        compiler_params=pltpu.CompilerParams(
            dimension_semantics=("parallel",), vmem_limit_bytes=VMEM_LIMIT),
        name="merge",
    )(x1, att_sb, att_ca, gates, gates, w_o_sb, w_o_ca, w_out)


def kernel(x, ffn1_norm, ffn1_w_gate, ffn1_w_up, ffn1_w_down, mix_norm, w_in, b_gate,
           q_norm_ca, k_norm_ca, rel_bias, w_o_sb, w_o_ca, w_out, ffn2_norm,
           ffn2_w_gate, ffn2_w_up, ffn2_w_down, final_norm):
    b, s, d = x.shape
    depth = ffn1_norm.shape[0]
    bf = lambda w: w.astype(BF16)
    row = lambda v: v.reshape(1, -1)
    xf = x.reshape(b * s, d)
    for l in range(depth):
        x1, hn = _ffn(xf, row(ffn1_norm[l]), ffn1_w_gate[l], ffn1_w_up[l], ffn1_w_down[l],
                      row(mix_norm[l]), emit_residual=True)
        qkv = _inproj(hn, w_in[l], (COL_Q_SB, COL_K_SB, COL_V_SB, COL_V_CA), "plain")
        qk_ca = _inproj(hn, w_in[l], (COL_Q_CA, COL_K_CA), "head_norm",
                        jnp.stack([q_norm_ca[l], k_norm_ca[l]]))
        gates = _inproj(hn, w_in[l], tuple(range(COL_GATE, IN_COLS // WIDTH)), "gate", row(b_gate[l]))
        qkv3 = qkv.reshape(b, s, -1)
        att_sb = _sb_attention(qkv3, 0, 1, 2).reshape(b * s, WIDTH)
        att_ca = _ca_attention(qk_ca.reshape(b, s, -1), qkv3, 3, _ca_bias_row(rel_bias[l])).reshape(b * s, WIDTH)
        x2 = _merge(x1, att_sb, att_ca, gates, bf(w_o_sb[l]), bf(w_o_ca[l]), bf(w_out[l]))
        (xf,) = _ffn(x2, row(ffn2_norm[l]), ffn2_w_gate[l], ffn2_w_up[l], ffn2_w_down[l],
                     row(final_norm[l]), emit_residual=False)
    return xf.reshape(b, s, d)
```

```python
import functools

import jax
import jax.numpy as jnp
from jax import lax
from jax.experimental import pallas as pl
from jax.experimental.pallas import tpu as pltpu

D_MODEL = 2048
D_FF = 5632
HEAD_DIM = 128
N_HEADS = 8
WIDTH = N_HEADS * HEAD_DIM
CHUNK = 64
N_PAST_CHUNKS = 8
REL_CLIP_PAST = 128
N_REL = REL_CLIP_PAST + CHUNK
N_BRANCH = 2
IN_COLS = 6 * WIDTH + N_BRANCH * D_MODEL
EPS = 1e-6
SCALE = HEAD_DIM ** -0.5

COL_Q_SB, COL_K_SB, COL_V_SB, COL_Q_CA, COL_K_CA, COL_V_CA, COL_GATE = range(7)

V7X_VMEM_BYTES = 64 * 1024 * 1024
VMEM_LIMIT = V7X_VMEM_BYTES - 8 * 1024 * 1024

LOG_F32_UNDERFLOW = -104.0
LOG2_E = 1.4426950408889634
MASK_BIAS = -1e30

F32 = jnp.float32
BF16 = jnp.bfloat16


def _dot(a, b):
    return jnp.dot(a, b, preferred_element_type=F32)


def _dot_nt(a, b):
    return lax.dot_general(a, b, (((1,), (1,)), ((), ())), preferred_element_type=F32)


def _rmsnorm_f32(x, gain):
    return x * lax.rsqrt(jnp.mean(x * x, axis=-1, keepdims=True) + EPS) * gain


def _ffn_kernel(x_ref, nin_ref, wg_ref, wu_ref, wd_ref, nout_ref, *rest, emit_residual):
    if emit_residual:
        o_ref, hn_out_ref, hn_scr = rest
    else:
        o_ref, hn_scr = rest
    f = pl.program_id(1)

    @pl.when(f == 0)
    def _():
        x = x_ref[...]
        hn_scr[...] = _rmsnorm_f32(x, nin_ref[...]).astype(BF16)
        o_ref[...] = x

    h = hn_scr[...]
    g = _dot(h, wg_ref[...])
    u = _dot(h, wu_ref[...])
    a = (g * u * 0.5) / (1.0 + jnp.exp(-g))
    o_ref[...] += _dot(a.astype(BF16), wd_ref[...])

    @pl.when(f == pl.num_programs(1) - 1)
    def _():
        y = _rmsnorm_f32(o_ref[...], nout_ref[...])
        if emit_residual:
            hn_out_ref[...] = y.astype(BF16)
        else:
            o_ref[...] = y


def _ffn(x, norm_in, wg, wu, wd, norm_out, *, emit_residual, tm=512, tf=512):
    m = x.shape[0]
    grid = (m // tm, D_FF // tf)
    wg, wu, wd = wg.astype(BF16), wu.astype(BF16), wd.astype(BF16)
    out_shape = [jax.ShapeDtypeStruct((m, D_MODEL), F32)]
    out_specs = [pl.BlockSpec((tm, D_MODEL), lambda i, f: (i, 0))]
    if emit_residual:
        out_shape.append(jax.ShapeDtypeStruct((m, D_MODEL), BF16))
        out_specs.append(pl.BlockSpec((tm, D_MODEL), lambda i, f: (i, 0)))
    res = pl.pallas_call(
        functools.partial(_ffn_kernel, emit_residual=emit_residual),
        grid=grid,
        in_specs=[
            pl.BlockSpec((tm, D_MODEL), lambda i, f: (i, 0)),
            pl.BlockSpec((1, D_MODEL), lambda i, f: (0, 0)),
            pl.BlockSpec((D_MODEL, tf), lambda i, f: (0, f)),
            pl.BlockSpec((D_MODEL, tf), lambda i, f: (0, f)),
            pl.BlockSpec((tf, D_MODEL), lambda i, f: (f, 0)),
            pl.BlockSpec((1, D_MODEL), lambda i, f: (0, 0)),
        ],
        out_specs=out_specs,
        out_shape=out_shape,
        scratch_shapes=[pltpu.VMEM((tm, D_MODEL), BF16)],
        compiler_params=pltpu.CompilerParams(
            dimension_semantics=("parallel", "arbitrary"), vmem_limit_bytes=VMEM_LIMIT),
        name="ffn_res" if emit_residual else "ffn_final",
    )(x, norm_in, wg, wu, wd, norm_out)
    return res


def _inproj_kernel(h_ref, w_ref, *rest, epilogue):
    if epilogue == "plain":
        o_ref, w_scr = rest
    else:
        aux_ref, o_ref, w_scr = rest

    @pl.when(pl.program_id(1) == 0)
    def _():
        w_scr[...] = w_ref[...].astype(BF16)

    r = _dot(h_ref[...], w_scr[...])
    if epilogue == "plain":
        o_ref[...] = r.astype(BF16)
    elif epilogue == "head_norm":
        gains = aux_ref[pl.ds(pl.program_id(0), 1), :]
        for hd in range(N_HEADS):
            sl = slice(hd * HEAD_DIM, (hd + 1) * HEAD_DIM)
            o_ref[:, sl] = _rmsnorm_f32(r[:, sl], gains).astype(BF16)
    else:
        assert epilogue == "gate"
        o_ref[...] = (1.0 / (1.0 + jnp.exp(-(r + aux_ref[...])))).astype(BF16)


def _inproj(hn, w_in, w_col_blocks, epilogue, aux=None, *, tm=1024):
    m = hn.shape[0]
    n = len(w_col_blocks)
    first, last = w_col_blocks[0], w_col_blocks[-1]
    assert list(w_col_blocks[:-1]) == list(range(first, first + n - 1)) and last >= first + n - 1
    assert (aux is None) == (epilogue == "plain")
    aux_specs = {
        "plain": [],
        "gate": [pl.BlockSpec((1, WIDTH), lambda j, i: (0, j))],
        "head_norm": [pl.BlockSpec((n, HEAD_DIM), lambda j, i: (0, 0))],
    }[epilogue]
    return pl.pallas_call(
        functools.partial(_inproj_kernel, epilogue=epilogue),
        grid=(n, m // tm),
        in_specs=[
            pl.BlockSpec((tm, D_MODEL), lambda j, i: (i, 0)),
            pl.BlockSpec((D_MODEL, WIDTH), lambda j, i: (0, first + j + (last - first - n + 1) * (j // (n - 1)))),
            *aux_specs,
        ],
        out_specs=pl.BlockSpec((tm, WIDTH), lambda j, i: (i, j)),
        out_shape=jax.ShapeDtypeStruct((m, n * WIDTH), BF16),
        scratch_shapes=[pltpu.VMEM((D_MODEL, WIDTH), BF16)],
        compiler_params=pltpu.CompilerParams(
            dimension_semantics=("arbitrary", "arbitrary"), vmem_limit_bytes=VMEM_LIMIT),
        name="inproj_" + epilogue,
    )(hn, w_in, *([] if aux is None else [aux]))


SB_BLK = 256
SB_HEADS = 4


def _sb_kernel(q_ref, k_ref, v_ref, o_ref):
    blk = SB_BLK
    row = lax.broadcasted_iota(jnp.int32, (blk, blk), 0)
    col = lax.broadcasted_iota(jnp.int32, (blk, blk), 1)
    later_sum = (row > col).astype(BF16)
    causal = col < row

    def sweep_block(qs, kb, state, diagonal):
        start = pl.multiple_of(kb * blk, blk)
        log_keeps, log_betas, splits = [], [], []
        for hd in range(SB_HEADS):
            lanes = slice(hd * HEAD_DIM, (hd + 1) * HEAD_DIM)
            nz = _dot_nt(qs[hd], k_ref[pl.ds(start, blk), lanes]) * (-SCALE * LOG2_E)
            log_keep = jnp.minimum(nz, 0.0) - jnp.log2(1.0 + jnp.exp2(-jnp.abs(nz)))
            log_betas.append(log_keep - nz)
            if diagonal:
                log_keep = jnp.where(causal, log_keep, 0.0)
            log_keeps.append(log_keep)
            hi = log_keep.astype(BF16)
            splits += [hi, (log_keep - hi.astype(F32)).astype(BF16)]
        later_all = _dot(jnp.concatenate(splits, axis=0), later_sum)
        new_state = []
        for hd in range(SB_HEADS):
            run, acc = state[hd]
            lanes = slice(hd * HEAD_DIM, (hd + 1) * HEAD_DIM)
            later = later_all[2 * hd * blk:(2 * hd + 1) * blk] + later_all[(2 * hd + 1) * blk:(2 * hd + 2) * blk]
            w = jnp.exp2(log_betas[hd] + later + run)
            if diagonal:
                w = jnp.where(causal, w, 0.0)
            acc = acc + _dot(w.astype(BF16), v_ref[pl.ds(start, blk), lanes])
            run = run + jnp.sum(log_keeps[hd], axis=1, keepdims=True)
            new_state.append((run, acc))
        return tuple(new_state)

    def query_tile(qi, carry):
        q0 = pl.multiple_of(qi * blk, blk)
        qs = [q_ref[pl.ds(q0, blk), hd * HEAD_DIM:(hd + 1) * HEAD_DIM] for hd in range(SB_HEADS)]
        run0 = jnp.zeros((blk, 1), F32)
        acc0 = jnp.zeros((blk, HEAD_DIM), F32)
        state = sweep_block(qs, qi, ((run0, acc0),) * SB_HEADS, True)

        def cond(c):
            kb, st = c
            worst = functools.reduce(jnp.maximum, [run for run, _ in st])
            return jnp.logical_and(kb >= 0, jnp.max(worst) > LOG_F32_UNDERFLOW * LOG2_E)

        def body(c):
            kb, st = c
            return kb - 1, sweep_block(qs, kb, st, False)

        _, state = lax.while_loop(cond, body, (qi - 1, state))
        for hd in range(SB_HEADS):
            o_ref[pl.ds(q0, blk), hd * HEAD_DIM:(hd + 1) * HEAD_DIM] = state[hd][1].astype(BF16)
        return carry

    lax.fori_loop(0, q_ref.shape[0] // blk, query_tile, 0)


def _sb_attention(qkv, q_col, k_col, v_col):
    b, s, _ = qkv.shape
    groups = N_HEADS // SB_HEADS
    blk = (None, s, SB_HEADS * HEAD_DIM)

    def col(c):
        return lambda bi, g: (bi, 0, c * groups + g)

    return pl.pallas_call(
        _sb_kernel,
        grid=(b, groups),
        in_specs=[pl.BlockSpec(blk, col(q_col)), pl.BlockSpec(blk, col(k_col)), pl.BlockSpec(blk, col(v_col))],
        out_specs=pl.BlockSpec(blk, col(0)),
        out_shape=jax.ShapeDtypeStruct((b, s, WIDTH), BF16),
        compiler_params=pltpu.CompilerParams(
            dimension_semantics=("parallel", "parallel"), vmem_limit_bytes=VMEM_LIMIT),
        name="sb_attn",
    )(qkv, qkv, qkv)


CA_TQ = 256
CA_KBLKS = 1 + N_PAST_CHUNKS * CHUNK // CA_TQ
CA_WIN = CA_KBLKS * CA_TQ
CA_ROLL = 1024


def _ca_bias_row(rel_bias):
    far = N_PAST_CHUNKS * CHUNK - REL_CLIP_PAST + 1
    past = jnp.broadcast_to(rel_bias[:, N_REL - 1:], (N_HEADS, far))
    ramp = rel_bias[:, N_REL - 2::-1]
    future = jnp.broadcast_to(rel_bias[:, :1], (N_HEADS, CA_WIN + 1 - far - (N_REL - 1)))
    wrap = jnp.broadcast_to(rel_bias[:, N_REL - 1:], (N_HEADS, CA_ROLL - CA_WIN - 1))
    return jnp.concatenate([past, ramp, future, wrap], axis=1).astype(F32)[:, None, :]


def _ca_kernel(q_ref, k_ref, v_ref, brow_ref, o_ref, bias_scr):
    @pl.when(pl.program_id(1) == 0)
    def _():
        rows = jnp.broadcast_to(brow_ref[...], (CA_TQ, CA_ROLL))
        table = pltpu.roll(rows, 0, 1, stride=1, stride_axis=0)[:, :CA_WIN]
        q_chunk = lax.broadcasted_iota(jnp.int32, (CA_TQ, CA_WIN), 0) // CHUNK
        k_chunk = lax.broadcasted_iota(jnp.int32, (CA_TQ, CA_WIN), 1) // CHUNK
        in_band = (k_chunk >= q_chunk) & (k_chunk <= q_chunk + N_PAST_CHUNKS)
        bias_scr[...] = jnp.where(in_band, table, MASK_BIAS)

    def query_tile(t, carry):
        q0 = pl.multiple_of(t * CA_TQ, CA_TQ)
        q = q_ref[pl.ds(q0, CA_TQ), :]
        scores, starts = [], []
        for r in range(CA_KBLKS):
            kb = t - (CA_KBLKS - 1) + r
            k0 = pl.multiple_of(jnp.maximum(kb, 0) * CA_TQ, CA_TQ)
            s = _dot_nt(q, k_ref[pl.ds(k0, CA_TQ), :]) * SCALE + bias_scr[:, r * CA_TQ:(r + 1) * CA_TQ]
            if r < CA_KBLKS - 1:
                s = jnp.where(kb >= 0, s, MASK_BIAS)
            scores.append(s)
            starts.append(k0)
        m = functools.reduce(jnp.maximum, [jnp.max(s, axis=1, keepdims=True) for s in scores])
        ps = [jnp.exp(s - m) for s in scores]
        denom = functools.reduce(jnp.add, [jnp.sum(p, axis=1, keepdims=True) for p in ps])
        o = functools.reduce(
            jnp.add, [_dot(p.astype(BF16), v_ref[pl.ds(k0, CA_TQ), :]) for p, k0 in zip(ps, starts)])
        o_ref[pl.ds(q0, CA_TQ), :] = (o / denom).astype(BF16)
        return carry

    lax.fori_loop(0, q_ref.shape[0] // CA_TQ, query_tile, 0, unroll=4)


def _ca_attention(qk, v, v_col, bias_row):
    b, s, _ = qk.shape
    blk = (None, s, HEAD_DIM)

    def col(c):
        return lambda h, bi: (bi, 0, c * N_HEADS + h)

    return pl.pallas_call(
        _ca_kernel,
        grid=(N_HEADS, b),
        in_specs=[
            pl.BlockSpec(blk, col(0)),
            pl.BlockSpec(blk, col(1)),
            pl.BlockSpec(blk, col(v_col)),
            pl.BlockSpec((None, 1, CA_ROLL), lambda h, bi: (h, 0, 0)),
        ],
        out_specs=pl.BlockSpec(blk, lambda h, bi: (bi, 0, h)),
        out_shape=jax.ShapeDtypeStruct((b, s, WIDTH), BF16),
        scratch_shapes=[pltpu.VMEM((CA_TQ, CA_WIN), F32)],
        compiler_params=pltpu.CompilerParams(
            dimension_semantics=("arbitrary", "arbitrary"), vmem_limit_bytes=VMEM_LIMIT),
        name="ca_attn",
    )(qk, qk, v, bias_row)


def _merge_kernel(x_ref, a_ref, c_ref, ga_ref, gb_ref, woa_ref, wob_ref, wout_ref, o_ref):
    ya = _dot(a_ref[...], woa_ref[...])
    yb = _dot(c_ref[...], wob_ref[...])
    merged = ga_ref[...].astype(F32) * ya + gb_ref[...].astype(F32) * yb
    o_ref[...] = x_ref[...] + _dot(merged.astype(BF16), wout_ref[...])


def _merge(x1, att_sb, att_ca, gates, w_o_sb, w_o_ca, w_out, *, tm=512):
    m = x1.shape[0]
    resident = dict(pipeline_mode=pl.Buffered(1))
    return pl.pallas_call(
        _merge_kernel,
        grid=(m // tm,),
        in_specs=[
            pl.BlockSpec((tm, D_MODEL), lambda i: (i, 0)),
            pl.BlockSpec((tm, WIDTH), lambda i: (i, 0)),
            pl.BlockSpec((tm, WIDTH), lambda i: (i, 0)),
            pl.BlockSpec((tm, D_MODEL), lambda i: (i, 0)),
            pl.BlockSpec((tm, D_MODEL), lambda i: (i, 1)),
            pl.BlockSpec((WIDTH, D_MODEL), lambda i: (0, 0), **resident),
            pl.BlockSpec((WIDTH, D_MODEL), lambda i: (0, 0), **resident),
            pl.BlockSpec((D_MODEL, D_MODEL), lambda i: (0, 0), **resident),
        ],
        out_specs=pl.BlockSpec((tm, D_MODEL), lambda i: (i, 0)),
        out_shape=jax.ShapeDtypeStruct((m, D_MODEL), F32),
        compiler_params=pltpu.CompilerParams(
            dimension_semantics=("parallel",), vmem_limit_bytes=VMEM_LIMIT),
        name="merge",
    )(x1, att_sb, att_ca, gates, gates, w_o_sb, w_o_ca, w_out)


def kernel(x, ffn1_norm, ffn1_w_gate, ffn1_w_up, ffn1_w_down, mix_norm, w_in, b_gate,
           q_norm_ca, k_norm_ca, rel_bias, w_o_sb, w_o_ca, w_out, ffn2_norm,
           ffn2_w_gate, ffn2_w_up, ffn2_w_down, final_norm):
    b, s, d = x.shape
    depth = ffn1_norm.shape[0]
    bf = lambda w: w.astype(BF16)
    row = lambda v: v.reshape(1, -1)
    xf = x.reshape(b * s, d)
    for l in range(depth):
        x1, hn = _ffn(xf, row(ffn1_norm[l]), ffn1_w_gate[l], ffn1_w_up[l], ffn1_w_down[l],
                      row(mix_norm[l]), emit_residual=True)
        qkv = _inproj(hn, w_in[l], (COL_Q_SB, COL_K_SB, COL_V_SB, COL_V_CA), "plain")
        qk_ca = _inproj(hn, w_in[l], (COL_Q_CA, COL_K_CA), "head_norm",
                        jnp.stack([q_norm_ca[l], k_norm_ca[l]]))
        gates = _inproj(hn, w_in[l], tuple(range(COL_GATE, IN_COLS // WIDTH)), "gate", row(b_gate[l]))
        qkv3 = qkv.reshape(b, s, -1)
        att_sb = _sb_attention(qkv3, 0, 1, 2).reshape(b * s, WIDTH)
        att_ca = _ca_attention(qk_ca.reshape(b, s, -1), qkv3, 3, _ca_bias_row(rel_bias[l])).reshape(b * s, WIDTH)
        x2 = _merge(x1, att_sb, att_ca, gates, bf(w_o_sb[l]), bf(w_o_ca[l]), bf(w_out[l]))
        (xf,) = _ffn(x2, row(ffn2_norm[l]), ffn2_w_gate[l], ffn2_w_up[l], ffn2_w_down[l],
                     row(final_norm[l]), emit_residual=False)
    return xf.reshape(b, s, d)
```

```python
import functools

import jax
import jax.numpy as jnp
from jax import lax
from jax.experimental import pallas as pl
from jax.experimental.pallas import tpu as pltpu

D_MODEL = 2048
D_FF = 5632
HEAD_DIM = 128
N_HEADS = 8
WIDTH = N_HEADS * HEAD_DIM
CHUNK = 64
N_PAST_CHUNKS = 8
REL_CLIP_PAST = 128
N_REL = REL_CLIP_PAST + CHUNK
N_BRANCH = 2
IN_COLS = 6 * WIDTH + N_BRANCH * D_MODEL
EPS = 1e-6
SCALE = HEAD_DIM ** -0.5

COL_Q_SB, COL_K_SB, COL_V_SB, COL_Q_CA, COL_K_CA, COL_V_CA, COL_GATE = range(7)

V7X_VMEM_BYTES = 64 * 1024 * 1024
VMEM_LIMIT = V7X_VMEM_BYTES - 8 * 1024 * 1024

LOG_F32_UNDERFLOW = -104.0
LOG2_E = 1.4426950408889634
MASK_BIAS = -1e30

F32 = jnp.float32
BF16 = jnp.bfloat16


def _dot(a, b):
    return jnp.dot(a, b, preferred_element_type=F32)


def _dot_nt(a, b):
    return lax.dot_general(a, b, (((1,), (1,)), ((), ())), preferred_element_type=F32)


def _rmsnorm_f32(x, gain):
    return x * lax.rsqrt(jnp.mean(x * x, axis=-1, keepdims=True) + EPS) * gain


FFN_TM = 512
FFN_TF = 512
FFN_SLOTS = 3
FFN_TILES = D_FF // FFN_TF


def _ffn_kernel(x_ref, nin_ref, wg_hbm, wu_hbm, wd_hbm, nout_ref, *rest, emit_residual):
    if emit_residual:
        o_ref, hn_out_ref, hn_scr, wg_buf, wu_buf, wd_buf, sem = rest
    else:
        o_ref, hn_scr, wg_buf, wu_buf, wd_buf, sem = rest
    i = pl.program_id(0)

    def tile_copies(f):
        slot = f % FFN_SLOTS
        span = pl.ds(f * FFN_TF, FFN_TF)
        return (
            pltpu.make_async_copy(wg_hbm.at[:, span], wg_buf.at[slot], sem.at[0, slot]),
            pltpu.make_async_copy(wu_hbm.at[:, span], wu_buf.at[slot], sem.at[1, slot]),
            pltpu.make_async_copy(wd_hbm.at[span, :], wd_buf.at[slot], sem.at[2, slot]),
        )

    def start(f):
        for c in tile_copies(f):
            c.start()

    def wait(f):
        for c in tile_copies(f):
            c.wait()

    @pl.when(i == 0)
    def _():
        start(0)

    x = x_ref[...]
    hn_scr[...] = _rmsnorm_f32(x, nin_ref[...]).astype(BF16)
    o_ref[...] = x

    for f in range(FFN_TILES):
        if f + 1 < FFN_TILES:
            start(f + 1)
        else:
            @pl.when(i + 1 < pl.num_programs(0))
            def _():
                start(0)
        wait(f)
        slot = f % FFN_SLOTS
        h = hn_scr[...]
        g = _dot(h, wg_buf[slot])
        u = _dot(h, wu_buf[slot])
        a = (g * u * 0.5) / (1.0 + jnp.exp(-g))
        o_ref[...] += _dot(a.astype(BF16), wd_buf[slot])

    y = _rmsnorm_f32(o_ref[...], nout_ref[...])
    if emit_residual:
        hn_out_ref[...] = y.astype(BF16)
    else:
        o_ref[...] = y


def _ffn(x, norm_in, wg, wu, wd, norm_out, *, emit_residual):
    m = x.shape[0]
    tm, tf = FFN_TM, FFN_TF
    wg, wu, wd = wg.astype(BF16), wu.astype(BF16), wd.astype(BF16)
    out_shape = [jax.ShapeDtypeStruct((m, D_MODEL), F32)]
    out_specs = [pl.BlockSpec((tm, D_MODEL), lambda i: (i, 0))]
    if emit_residual:
        out_shape.append(jax.ShapeDtypeStruct((m, D_MODEL), BF16))
        out_specs.append(pl.BlockSpec((tm, D_MODEL), lambda i: (i, 0)))
    res = pl.pallas_call(
        functools.partial(_ffn_kernel, emit_residual=emit_residual),
        grid=(m // tm,),
        in_specs=[
            pl.BlockSpec((tm, D_MODEL), lambda i: (i, 0)),
            pl.BlockSpec((1, D_MODEL), lambda i: (0, 0)),
            pl.BlockSpec(memory_space=pl.ANY),
            pl.BlockSpec(memory_space=pl.ANY),
            pl.BlockSpec(memory_space=pl.ANY),
            pl.BlockSpec((1, D_MODEL), lambda i: (0, 0)),
        ],
        out_specs=out_specs,
        out_shape=out_shape,
        scratch_shapes=[
            pltpu.VMEM((tm, D_MODEL), BF16),
            pltpu.VMEM((FFN_SLOTS, D_MODEL, tf), BF16),
            pltpu.VMEM((FFN_SLOTS, D_MODEL, tf), BF16),
            pltpu.VMEM((FFN_SLOTS, tf, D_MODEL), BF16),
            pltpu.SemaphoreType.DMA((3, FFN_SLOTS)),
        ],
        compiler_params=pltpu.CompilerParams(
            dimension_semantics=("arbitrary",), vmem_limit_bytes=VMEM_LIMIT),
        name="ffn_res" if emit_residual else "ffn_final",
    )(x, norm_in, wg, wu, wd, norm_out)
    return res


def _inproj_kernel(h_ref, w_ref, *rest, epilogue):
    if epilogue == "plain":
        o_ref, w_scr = rest
    else:
        aux_ref, o_ref, w_scr = rest

    @pl.when(pl.program_id(1) == 0)
    def _():
        w_scr[...] = w_ref[...].astype(BF16)

    r = _dot(h_ref[...], w_scr[...])
    if epilogue == "plain":
        o_ref[...] = r.astype(BF16)
    elif epilogue == "head_norm":
        gains = aux_ref[pl.ds(pl.program_id(0), 1), :]
        for hd in range(N_HEADS):
            sl = slice(hd * HEAD_DIM, (hd + 1) * HEAD_DIM)
            o_ref[:, sl] = _rmsnorm_f32(r[:, sl], gains).astype(BF16)
    else:
        assert epilogue == "gate"
        o_ref[...] = (1.0 / (1.0 + jnp.exp(-(r + aux_ref[...])))).astype(BF16)


def _inproj(hn, w_in, w_col_blocks, epilogue, aux=None, *, tm=1024):
    m = hn.shape[0]
    n = len(w_col_blocks)
    first, last = w_col_blocks[0], w_col_blocks[-1]
    assert list(w_col_blocks[:-1]) == list(range(first, first + n - 1)) and last >= first + n - 1
    assert (aux is None) == (epilogue == "plain")
    aux_specs = {
        "plain": [],
        "gate": [pl.BlockSpec((1, WIDTH), lambda j, i: (0, j))],
        "head_norm": [pl.BlockSpec((n, HEAD_DIM), lambda j, i: (0, 0))],
    }[epilogue]
    return pl.pallas_call(
        functools.partial(_inproj_kernel, epilogue=epilogue),
        grid=(n, m // tm),
        in_specs=[
            pl.BlockSpec((tm, D_MODEL), lambda j, i: (i, 0)),
            pl.BlockSpec((D_MODEL, WIDTH), lambda j, i: (0, first + j + (last - first - n + 1) * (j // (n - 1)))),
            *aux_specs,
        ],
        out_specs=pl.BlockSpec((tm, WIDTH), lambda j, i: (i, j)),
        out_shape=jax.ShapeDtypeStruct((m, n * WIDTH), BF16),
        scratch_shapes=[pltpu.VMEM((D_MODEL, WIDTH), BF16)],
        compiler_params=pltpu.CompilerParams(
            dimension_semantics=("arbitrary", "arbitrary"), vmem_limit_bytes=VMEM_LIMIT),
        name="inproj_" + epilogue,
    )(hn, w_in, *([] if aux is None else [aux]))


SB_BLK = 256
SB_HEADS = 4


def _sb_kernel(q_ref, k_ref, v_ref, o_ref):
    blk = SB_BLK
    row = lax.broadcasted_iota(jnp.int32, (blk, blk), 0)
    col = lax.broadcasted_iota(jnp.int32, (blk, blk), 1)
    later_sum = (row > col).astype(BF16)
    causal = col < row

    def sweep_block(qs, kb, state, diagonal):
        start = pl.multiple_of(kb * blk, blk)
        log_keeps, log_betas, splits = [], [], []
        for hd in range(SB_HEADS):
            lanes = slice(hd * HEAD_DIM, (hd + 1) * HEAD_DIM)
            nz = _dot_nt(qs[hd], k_ref[pl.ds(start, blk), lanes]) * (-SCALE * LOG2_E)
            log_keep = jnp.minimum(nz, 0.0) - jnp.log2(1.0 + jnp.exp2(-jnp.abs(nz)))
            log_betas.append(log_keep - nz)
            if diagonal:
                log_keep = jnp.where(causal, log_keep, 0.0)
            log_keeps.append(log_keep)
            hi = log_keep.astype(BF16)
            splits += [hi, (log_keep - hi.astype(F32)).astype(BF16)]
        later_all = _dot(jnp.concatenate(splits, axis=0), later_sum)
        new_state = []
        for hd in range(SB_HEADS):
            run, acc = state[hd]
            lanes = slice(hd * HEAD_DIM, (hd + 1) * HEAD_DIM)
            later = later_all[2 * hd * blk:(2 * hd + 1) * blk] + later_all[(2 * hd + 1) * blk:(2 * hd + 2) * blk]
            w = jnp.exp2(log_betas[hd] + later + run)
            if diagonal:
                w = jnp.where(causal, w, 0.0)
            acc = acc + _dot(w.astype(BF16), v_ref[pl.ds(start, blk), lanes])
            run = run + jnp.sum(log_keeps[hd], axis=1, keepdims=True)
            new_state.append((run, acc))
        return tuple(new_state)

    def query_tile(qi, carry):
        q0 = pl.multiple_of(qi * blk, blk)
        qs = [q_ref[pl.ds(q0, blk), hd * HEAD_DIM:(hd + 1) * HEAD_DIM] for hd in range(SB_HEADS)]
        run0 = jnp.zeros((blk, 1), F32)
        acc0 = jnp.zeros((blk, HEAD_DIM), F32)
        state = sweep_block(qs, qi, ((run0, acc0),) * SB_HEADS, True)

        def cond(c):
            kb, st = c
            worst = functools.reduce(jnp.maximum, [run for run, _ in st])
            return jnp.logical_and(kb >= 0, jnp.max(worst) > LOG_F32_UNDERFLOW * LOG2_E)

        def body(c):
            kb, st = c
            return kb - 1, sweep_block(qs, kb, st, False)

        _, state = lax.while_loop(cond, body, (qi - 1, state))
        for hd in range(SB_HEADS):
            o_ref[pl.ds(q0, blk), hd * HEAD_DIM:(hd + 1) * HEAD_DIM] = state[hd][1].astype(BF16)
        return carry

    lax.fori_loop(0, q_ref.shape[0] // blk, query_tile, 0)


def _sb_attention(qkv, q_col, k_col, v_col):
    b, s, _ = qkv.shape
    groups = N_HEADS // SB_HEADS
    blk = (None, s, SB_HEADS * HEAD_DIM)

    def col(c):
        return lambda bi, g: (bi, 0, c * groups + g)

    return pl.pallas_call(
        _sb_kernel,
        grid=(b, groups),
        in_specs=[pl.BlockSpec(blk, col(q_col)), pl.BlockSpec(blk, col(k_col)), pl.BlockSpec(blk, col(v_col))],
        out_specs=pl.BlockSpec(blk, col(0)),
        out_shape=jax.ShapeDtypeStruct((b, s, WIDTH), BF16),
        compiler_params=pltpu.CompilerParams(
            dimension_semantics=("parallel", "parallel"), vmem_limit_bytes=VMEM_LIMIT),
        name="sb_attn",
    )(qkv, qkv, qkv)


CA_TQ = 256
CA_KBLKS = 1 + N_PAST_CHUNKS * CHUNK // CA_TQ
CA_WIN = CA_KBLKS * CA_TQ
CA_ROLL = 1024


def _ca_bias_row(rel_bias):
    far = N_PAST_CHUNKS * CHUNK - REL_CLIP_PAST + 1
    past = jnp.broadcast_to(rel_bias[:, N_REL - 1:], (N_HEADS, far))
    ramp = rel_bias[:, N_REL - 2::-1]
    future = jnp.broadcast_to(rel_bias[:, :1], (N_HEADS, CA_WIN + 1 - far - (N_REL - 1)))
    wrap = jnp.broadcast_to(rel_bias[:, N_REL - 1:], (N_HEADS, CA_ROLL - CA_WIN - 1))
    return jnp.concatenate([past, ramp, future, wrap], axis=1).astype(F32)[:, None, :]


def _ca_kernel(q_ref, k_ref, v_ref, brow_ref, o_ref, bias_scr):
    @pl.when(pl.program_id(1) == 0)
    def _():
        rows = jnp.broadcast_to(brow_ref[...], (CA_TQ, CA_ROLL))
        table = pltpu.roll(rows, 0, 1, stride=1, stride_axis=0)[:, :CA_WIN]
        q_chunk = lax.broadcasted_iota(jnp.int32, (CA_TQ, CA_WIN), 0) // CHUNK
        k_chunk = lax.broadcasted_iota(jnp.int32, (CA_TQ, CA_WIN), 1) // CHUNK
        in_band = (k_chunk >= q_chunk) & (k_chunk <= q_chunk + N_PAST_CHUNKS)
        bias_scr[...] = jnp.where(in_band, table, MASK_BIAS)

    def query_tile(t, carry):
        q0 = pl.multiple_of(t * CA_TQ, CA_TQ)
        q = q_ref[pl.ds(q0, CA_TQ), :]
        scores, starts = [], []
        for r in range(CA_KBLKS):
            kb = t - (CA_KBLKS - 1) + r
            k0 = pl.multiple_of(jnp.maximum(kb, 0) * CA_TQ, CA_TQ)
            s = _dot_nt(q, k_ref[pl.ds(k0, CA_TQ), :]) * SCALE + bias_scr[:, r * CA_TQ:(r + 1) * CA_TQ]
            if r < CA_KBLKS - 1:
                s = jnp.where(kb >= 0, s, MASK_BIAS)
            scores.append(s)
            starts.append(k0)
        m = functools.reduce(jnp.maximum, [jnp.max(s, axis=1, keepdims=True) for s in scores])
        ps = [jnp.exp(s - m) for s in scores]
        denom = functools.reduce(jnp.add, [jnp.sum(p, axis=1, keepdims=True) for p in ps])
        o = functools.reduce(
            jnp.add, [_dot(p.astype(BF16), v_ref[pl.ds(k0, CA_TQ), :]) for p, k0 in zip(ps, starts)])
        o_ref[pl.ds(q0, CA_TQ), :] = (o / denom).astype(BF16)
        return carry

    lax.fori_loop(0, q_ref.shape[0] // CA_TQ, query_tile, 0, unroll=4)


def _ca_attention(qk, v, v_col, bias_row):
    b, s, _ = qk.shape
    blk = (None, s, HEAD_DIM)

    def col(c):
        return lambda h, bi: (bi, 0, c * N_HEADS + h)

    return pl.pallas_call(
        _ca_kernel,
        grid=(N_HEADS, b),
        in_specs=[
            pl.BlockSpec(blk, col(0)),
            pl.BlockSpec(blk, col(1)),
            pl.BlockSpec(blk, col(v_col)),
            pl.BlockSpec((None, 1, CA_ROLL), lambda h, bi: (h, 0, 0)),
        ],
        out_specs=pl.BlockSpec(blk, lambda h, bi: (bi, 0, h)),
        out_shape=jax.ShapeDtypeStruct((b, s, WIDTH), BF16),
        scratch_shapes=[pltpu.VMEM((CA_TQ, CA_WIN), F32)],
        compiler_params=pltpu.CompilerParams(
            dimension_semantics=("arbitrary", "arbitrary"), vmem_limit_bytes=VMEM_LIMIT),
        name="ca_attn",
    )(qk, qk, v, bias_row)


def _merge_kernel(x_ref, a_ref, c_ref, ga_ref, gb_ref, woa_ref, wob_ref, wout_ref, o_ref):
    ya = _dot(a_ref[...], woa_ref[...])
    yb = _dot(c_ref[...], wob_ref[...])
    merged = ga_ref[...].astype(F32) * ya + gb_ref[...].astype(F32) * yb
    o_ref[...] = x_ref[...] + _dot(merged.astype(BF16), wout_ref[...])


def _merge(x1, att_sb, att_ca, gates, w_o_sb, w_o_ca, w_out, *, tm=512):
    m = x1.shape[0]
    resident = dict(pipeline_mode=pl.Buffered(1))
    return pl.pallas_call(
        _merge_kernel,
        grid=(m // tm,),
        in_specs=[
            pl.BlockSpec((tm, D_MODEL), lambda i: (i, 0)),
            pl.BlockSpec((tm, WIDTH), lambda i: (i, 0)),
            pl.BlockSpec((tm, WIDTH), lambda i: (i, 0)),
            pl.BlockSpec((tm, D_MODEL), lambda i: (i, 0)),
            pl.BlockSpec((tm, D_MODEL), lambda i: (i, 1)),
            pl.BlockSpec((WIDTH, D_MODEL), lambda i: (0, 0), **resident),
            pl.BlockSpec((WIDTH, D_MODEL), lambda i: (0, 0), **resident),
            pl.BlockSpec((D_MODEL, D_MODEL), lambda i: (0, 0), **resident),
        ],
        out_specs=pl.BlockSpec((tm, D_MODEL), lambda i: (i, 0)),
        out_shape=jax.ShapeDtypeStruct((m, D_MODEL), F32),
        compiler_params=pltpu.CompilerParams(
            dimension_semantics=("parallel",), vmem_limit_bytes=VMEM_LIMIT),
        name="merge",
    )(x1, att_sb, att_ca, gates, gates, w_o_sb, w_o_ca, w_out)


def kernel(x, ffn1_norm, ffn1_w_gate, ffn1_w_up, ffn1_w_down, mix_norm, w_in, b_gate,
           q_norm_ca, k_norm_ca, rel_bias, w_o_sb, w_o_ca, w_out, ffn2_norm,
           ffn2_w_gate, ffn2_w_up, ffn2_w_down, final_norm):
    b, s, d = x.shape
    depth = ffn1_norm.shape[0]
    bf = lambda w: w.astype(BF16)
    row = lambda v: v.reshape(1, -1)
    xf = x.reshape(b * s, d)
    for l in range(depth):
        x1, hn = _ffn(xf, row(ffn1_norm[l]), ffn1_w_gate[l], ffn1_w_up[l], ffn1_w_down[l],
                      row(mix_norm[l]), emit_residual=True)
        qkv = _inproj(hn, w_in[l], (COL_Q_SB, COL_K_SB, COL_V_SB, COL_V_CA), "plain")
        qk_ca = _inproj(hn, w_in[l], (COL_Q_CA, COL_K_CA), "head_norm",
                        jnp.stack([q_norm_ca[l], k_norm_ca[l]]))
        gates = _inproj(hn, w_in[l], tuple(range(COL_GATE, IN_COLS // WIDTH)), "gate", row(b_gate[l]))
        qkv3 = qkv.reshape(b, s, -1)
        att_sb = _sb_attention(qkv3, 0, 1, 2).reshape(b * s, WIDTH)
        att_ca = _ca_attention(qk_ca.reshape(b, s, -1), qkv3, 3, _ca_bias_row(rel_bias[l])).reshape(b * s, WIDTH)
        x2 = _merge(x1, att_sb, att_ca, gates, bf(w_o_sb[l]), bf(w_o_ca[l]), bf(w_out[l]))
        (xf,) = _ffn(x2, row(ffn2_norm[l]), ffn2_w_gate[l], ffn2_w_up[l], ffn2_w_down[l],
                     row(final_norm[l]), emit_residual=False)
    return xf.reshape(b, s, d)
```

```python
import functools

import jax
import jax.numpy as jnp
from jax import lax
from jax.experimental import pallas as pl
from jax.experimental.pallas import tpu as pltpu

D_MODEL = 2048
D_FF = 5632
HEAD_DIM = 128
N_HEADS = 8
WIDTH = N_HEADS * HEAD_DIM
CHUNK = 64
N_PAST_CHUNKS = 8
REL_CLIP_PAST = 128
N_REL = REL_CLIP_PAST + CHUNK
N_BRANCH = 2
IN_COLS = 6 * WIDTH + N_BRANCH * D_MODEL
EPS = 1e-6
SCALE = HEAD_DIM ** -0.5

COL_Q_SB, COL_K_SB, COL_V_SB, COL_Q_CA, COL_K_CA, COL_V_CA, COL_GATE = range(7)

V7X_VMEM_BYTES = 64 * 1024 * 1024
VMEM_LIMIT = V7X_VMEM_BYTES - 8 * 1024 * 1024

LOG_F32_UNDERFLOW = -104.0
LOG2_E = 1.4426950408889634
MASK_BIAS = -1e30

F32 = jnp.float32
BF16 = jnp.bfloat16
BF16_SUBLANES = 16


def _dot(a, b):
    return jnp.dot(a, b, preferred_element_type=F32)


def _dot_nt(a, b):
    return lax.dot_general(a, b, (((1,), (1,)), ((), ())), preferred_element_type=F32)


def _rmsnorm_f32(x, gain):
    return x * lax.rsqrt(jnp.mean(x * x, axis=-1, keepdims=True) + EPS) * gain


def _ffn_kernel(x_ref, nin_ref, wg_ref, wu_ref, wd_ref, nout_ref, *rest, emit_residual):
    if emit_residual:
        o_ref, hn_out_ref, hn_scr = rest
    else:
        o_ref, hn_scr = rest
    f = pl.program_id(1)

    @pl.when(f == 0)
    def _():
        x = x_ref[...]
        hn_scr[...] = _rmsnorm_f32(x, nin_ref[...]).astype(BF16)
        o_ref[...] = x

    h = hn_scr[...]
    g = _dot(h, wg_ref[...])
    u = _dot(h, wu_ref[...])
    a = (g * u * 0.5) / (1.0 + jnp.exp(-g))
    o_ref[...] += _dot(a.astype(BF16), wd_ref[...])

    @pl.when(f == pl.num_programs(1) - 1)
    def _():
        y = _rmsnorm_f32(o_ref[...], nout_ref[...])
        if emit_residual:
            hn_out_ref[...] = y.astype(BF16)
        else:
            o_ref[...] = y


def _ffn(x, norm_in, wg, wu, wd, norm_out, *, emit_residual, tm=512, tf=512):
    m = x.shape[0]
    grid = (m // tm, D_FF // tf)
    wg, wu, wd = wg.astype(BF16), wu.astype(BF16), wd.astype(BF16)
    out_shape = [jax.ShapeDtypeStruct((m, D_MODEL), F32)]
    out_specs = [pl.BlockSpec((tm, D_MODEL), lambda i, f: (i, 0))]
    if emit_residual:
        out_shape.append(jax.ShapeDtypeStruct((m, D_MODEL), BF16))
        out_specs.append(pl.BlockSpec((tm, D_MODEL), lambda i, f: (i, 0)))
    res = pl.pallas_call(
        functools.partial(_ffn_kernel, emit_residual=emit_residual),
        grid=grid,
        in_specs=[
            pl.BlockSpec((tm, D_MODEL), lambda i, f: (i, 0)),
            pl.BlockSpec((1, D_MODEL), lambda i, f: (0, 0)),
            pl.BlockSpec((D_MODEL, tf), lambda i, f: (0, f)),
            pl.BlockSpec((D_MODEL, tf), lambda i, f: (0, f)),
            pl.BlockSpec((tf, D_MODEL), lambda i, f: (f, 0)),
            pl.BlockSpec((1, D_MODEL), lambda i, f: (0, 0)),
        ],
        out_specs=out_specs,
        out_shape=out_shape,
        scratch_shapes=[pltpu.VMEM((tm, D_MODEL), BF16)],
        compiler_params=pltpu.CompilerParams(
            dimension_semantics=("parallel", "arbitrary"), vmem_limit_bytes=VMEM_LIMIT),
        name="ffn_res" if emit_residual else "ffn_final",
    )(x, norm_in, wg, wu, wd, norm_out)
    return res


def _inproj_kernel(h_ref, w_ref, *rest, epilogue, n_side):
    rest = list(rest)
    aux_ref = None if epilogue == "plain" else rest.pop(0)
    side_in = [rest.pop(0) for _ in range(n_side)]
    o_ref = rest.pop(0)
    side_out = [rest.pop(0) for _ in range(n_side)]
    (w_scr,) = rest

    @pl.when(pl.program_id(1) == 0)
    def _():
        w_scr[...] = w_ref[...].astype(BF16)

    for src, dst in zip(side_in, side_out):
        dst[...] = src[...].astype(BF16)

    r = _dot(h_ref[...], w_scr[...])
    if epilogue == "plain":
        o_ref[...] = r.astype(BF16)
    elif epilogue == "head_norm":
        gains = aux_ref[pl.ds(pl.program_id(0), 1), :]
        for hd in range(N_HEADS):
            sl = slice(hd * HEAD_DIM, (hd + 1) * HEAD_DIM)
            o_ref[:, sl] = _rmsnorm_f32(r[:, sl], gains).astype(BF16)
    else:
        assert epilogue == "gate"
        o_ref[...] = (1.0 / (1.0 + jnp.exp(-(r + aux_ref[...])))).astype(BF16)


def _inproj(hn, w_in, w_col_blocks, epilogue, aux=None, *, side_casts=(), tm=1024):
    m = hn.shape[0]
    n = len(w_col_blocks)
    row_tiles = m // tm
    steps = n * row_tiles
    first, last = w_col_blocks[0], w_col_blocks[-1]
    assert list(w_col_blocks[:-1]) == list(range(first, first + n - 1)) and last >= first + n - 1
    assert (aux is None) == (epilogue == "plain")
    aux_specs = {
        "plain": [],
        "gate": [pl.BlockSpec((1, WIDTH), lambda j, i: (0, j))],
        "head_norm": [pl.BlockSpec((n, HEAD_DIM), lambda j, i: (0, 0))],
    }[epilogue]
    side_specs = []
    for a in side_casts:
        rows, cols = a.shape
        slab = rows // steps
        assert slab * steps == rows and slab % BF16_SUBLANES == 0, (a.shape, steps)
        side_specs.append(pl.BlockSpec((slab, cols), lambda j, i: (j * row_tiles + i, 0)))
    res = pl.pallas_call(
        functools.partial(_inproj_kernel, epilogue=epilogue, n_side=len(side_casts)),
        grid=(n, row_tiles),
        in_specs=[
            pl.BlockSpec((tm, D_MODEL), lambda j, i: (i, 0)),
            pl.BlockSpec((D_MODEL, WIDTH), lambda j, i: (0, first + j + (last - first - n + 1) * (j // (n - 1)))),
            *aux_specs,
            *side_specs,
        ],
        out_specs=[pl.BlockSpec((tm, WIDTH), lambda j, i: (i, j)), *side_specs],
        out_shape=[jax.ShapeDtypeStruct((m, n * WIDTH), BF16),
                   *[jax.ShapeDtypeStruct(a.shape, BF16) for a in side_casts]],
        scratch_shapes=[pltpu.VMEM((D_MODEL, WIDTH), BF16)],
        compiler_params=pltpu.CompilerParams(
            dimension_semantics=("arbitrary", "arbitrary"), vmem_limit_bytes=VMEM_LIMIT),
        name="inproj_" + epilogue,
    )(hn, w_in, *([] if aux is None else [aux]), *side_casts)
    return res[0], res[1:]


SB_BLK = 256
SB_HEADS = 4


def _sb_kernel(q_ref, k_ref, v_ref, o_ref):
    blk = SB_BLK
    row = lax.broadcasted_iota(jnp.int32, (blk, blk), 0)
    col = lax.broadcasted_iota(jnp.int32, (blk, blk), 1)
    later_sum = (row > col).astype(BF16)
    causal = col < row

    def sweep_block(qs, kb, state, diagonal):
        start = pl.multiple_of(kb * blk, blk)
        log_keeps, log_betas, splits = [], [], []
        for hd in range(SB_HEADS):
            lanes = slice(hd * HEAD_DIM, (hd + 1) * HEAD_DIM)
            nz = _dot_nt(qs[hd], k_ref[pl.ds(start, blk), lanes]) * (-SCALE * LOG2_E)
            log_keep = jnp.minimum(nz, 0.0) - jnp.log2(1.0 + jnp.exp2(-jnp.abs(nz)))
            log_betas.append(log_keep - nz)
            if diagonal:
                log_keep = jnp.where(causal, log_keep, 0.0)
            log_keeps.append(log_keep)
            hi = log_keep.astype(BF16)
            splits += [hi, (log_keep - hi.astype(F32)).astype(BF16)]
        later_all = _dot(jnp.concatenate(splits, axis=0), later_sum)
        new_state = []
        for hd in range(SB_HEADS):
            run, acc = state[hd]
            lanes = slice(hd * HEAD_DIM, (hd + 1) * HEAD_DIM)
            later = later_all[2 * hd * blk:(2 * hd + 1) * blk] + later_all[(2 * hd + 1) * blk:(2 * hd + 2) * blk]
            w = jnp.exp2(log_betas[hd] + later + run)
            if diagonal:
                w = jnp.where(causal, w, 0.0)
            acc = acc + _dot(w.astype(BF16), v_ref[pl.ds(start, blk), lanes])
            run = run + jnp.sum(log_keeps[hd], axis=1, keepdims=True)
            new_state.append((run, acc))
        return tuple(new_state)

    def query_tile(qi, first):
        q0 = pl.multiple_of(qi * blk, blk)
        qs = [q_ref[pl.ds(q0, blk), hd * HEAD_DIM:(hd + 1) * HEAD_DIM] for hd in range(SB_HEADS)]
        run0 = jnp.zeros((blk, 1), F32)
        acc0 = jnp.zeros((blk, HEAD_DIM), F32)
        state = sweep_block(qs, qi, ((run0, acc0),) * SB_HEADS, True)
        if not first:
            state = sweep_block(qs, qi - 1, state, False)

            def cond(c):
                kb, st = c
                worst = functools.reduce(jnp.maximum, [run for run, _ in st])
                return jnp.logical_and(kb >= 0, jnp.max(worst) > LOG_F32_UNDERFLOW * LOG2_E)

            def body(c):
                kb, st = c
                return kb - 1, sweep_block(qs, kb, st, False)

            _, state = lax.while_loop(cond, body, (qi - 2, state))
        for hd in range(SB_HEADS):
            o_ref[pl.ds(q0, blk), hd * HEAD_DIM:(hd + 1) * HEAD_DIM] = state[hd][1].astype(BF16)

    query_tile(jnp.int32(0), True)
    lax.fori_loop(1, q_ref.shape[0] // blk, lambda qi, c: (query_tile(qi, False), c)[1], 0)


def _sb_attention(qkv, q_col, k_col, v_col):
    b, s, _ = qkv.shape
    groups = N_HEADS // SB_HEADS
    blk = (None, s, SB_HEADS * HEAD_DIM)

    def col(c):
        return lambda bi, g: (bi, 0, c * groups + g)

    return pl.pallas_call(
        _sb_kernel,
        grid=(b, groups),
        in_specs=[pl.BlockSpec(blk, col(q_col)), pl.BlockSpec(blk, col(k_col)), pl.BlockSpec(blk, col(v_col))],
        out_specs=pl.BlockSpec(blk, col(0)),
        out_shape=jax.ShapeDtypeStruct((b, s, WIDTH), BF16),
        compiler_params=pltpu.CompilerParams(
            dimension_semantics=("parallel", "parallel"), vmem_limit_bytes=VMEM_LIMIT),
        name="sb_attn",
    )(qkv, qkv, qkv)


CA_TQ = 256
CA_KBLKS = 1 + N_PAST_CHUNKS * CHUNK // CA_TQ
CA_WIN = CA_KBLKS * CA_TQ
CA_ROLL = 1024


def _ca_bias_row(rel_bias):
    far = N_PAST_CHUNKS * CHUNK - REL_CLIP_PAST + 1
    past = jnp.broadcast_to(rel_bias[:, N_REL - 1:], (N_HEADS, far))
    ramp = rel_bias[:, N_REL - 2::-1]
    future = jnp.broadcast_to(rel_bias[:, :1], (N_HEADS, CA_WIN + 1 - far - (N_REL - 1)))
    wrap = jnp.broadcast_to(rel_bias[:, N_REL - 1:], (N_HEADS, CA_ROLL - CA_WIN - 1))
    return jnp.concatenate([past, ramp, future, wrap], axis=1).astype(F32)[:, None, :]


def _ca_kernel(q_ref, k_ref, v_ref, brow_ref, o_ref, bias_scr):
    @pl.when(pl.program_id(1) == 0)
    def _():
        rows = jnp.broadcast_to(brow_ref[...], (CA_TQ, CA_ROLL))
        table = pltpu.roll(rows, 0, 1, stride=1, stride_axis=0)[:, :CA_WIN]
        q_chunk = lax.broadcasted_iota(jnp.int32, (CA_TQ, CA_WIN), 0) // CHUNK
        k_chunk = lax.broadcasted_iota(jnp.int32, (CA_TQ, CA_WIN), 1) // CHUNK
        in_band = (k_chunk >= q_chunk) & (k_chunk <= q_chunk + N_PAST_CHUNKS)
        bias_scr[...] = jnp.where(in_band, table, MASK_BIAS)

    def query_tile(t, carry):
        q0 = pl.multiple_of(t * CA_TQ, CA_TQ)
        q = q_ref[pl.ds(q0, CA_TQ), :]
        scores, starts = [], []
        for r in range(CA_KBLKS):
            kb = t - (CA_KBLKS - 1) + r
            k0 = pl.multiple_of(jnp.maximum(kb, 0) * CA_TQ, CA_TQ)
            s = _dot_nt(q, k_ref[pl.ds(k0, CA_TQ), :]) * SCALE + bias_scr[:, r * CA_TQ:(r + 1) * CA_TQ]
            if r < CA_KBLKS - 1:
                s = jnp.where(kb >= 0, s, MASK_BIAS)
            scores.append(s)
            starts.append(k0)
        m = functools.reduce(jnp.maximum, [jnp.max(s, axis=1, keepdims=True) for s in scores])
        ps = [jnp.exp(s - m) for s in scores]
        denom = functools.reduce(jnp.add, [jnp.sum(p, axis=1, keepdims=True) for p in ps])
        o = functools.reduce(
            jnp.add, [_dot(p.astype(BF16), v_ref[pl.ds(k0, CA_TQ), :]) for p, k0 in zip(ps, starts)])
        o_ref[pl.ds(q0, CA_TQ), :] = (o / denom).astype(BF16)
        return carry

    lax.fori_loop(0, q_ref.shape[0] // CA_TQ, query_tile, 0, unroll=4)


def _ca_attention(qk, v, v_col, bias_row):
    b, s, _ = qk.shape
    blk = (None, s, HEAD_DIM)

    def col(c):
        return lambda h, bi: (bi, 0, c * N_HEADS + h)

    return pl.pallas_call(
        _ca_kernel,
        grid=(N_HEADS, b),
        in_specs=[
            pl.BlockSpec(blk, col(0)),
            pl.BlockSpec(blk, col(1)),
            pl.BlockSpec(blk, col(v_col)),
            pl.BlockSpec((None, 1, CA_ROLL), lambda h, bi: (h, 0, 0)),
        ],
        out_specs=pl.BlockSpec(blk, lambda h, bi: (bi, 0, h)),
        out_shape=jax.ShapeDtypeStruct((b, s, WIDTH), BF16),
        scratch_shapes=[pltpu.VMEM((CA_TQ, CA_WIN), F32)],
        compiler_params=pltpu.CompilerParams(
            dimension_semantics=("arbitrary", "arbitrary"), vmem_limit_bytes=VMEM_LIMIT),
        name="ca_attn",
    )(qk, qk, v, bias_row)


def _merge_kernel(x_ref, a_ref, c_ref, ga_ref, gb_ref, woa_ref, wob_ref, wout_ref, o_ref):
    ya = _dot(a_ref[...], woa_ref[...])
    yb = _dot(c_ref[...], wob_ref[...])
    merged = ga_ref[...].astype(F32) * ya + gb_ref[...].astype(F32) * yb
    o_ref[...] = x_ref[...] + _dot(merged.astype(BF16), wout_ref[...])


def _merge(x1, att_sb, att_ca, gates, w_o_sb, w_o_ca, w_out, *, tm=512):
    m = x1.shape[0]
    resident = dict(pipeline_mode=pl.Buffered(1))
    return pl.pallas_call(
        _merge_kernel,
        grid=(m // tm,),
        in_specs=[
            pl.BlockSpec((tm, D_MODEL), lambda i: (i, 0)),
            pl.BlockSpec((tm, WIDTH), lambda i: (i, 0)),
            pl.BlockSpec((tm, WIDTH), lambda i: (i, 0)),
            pl.BlockSpec((tm, D_MODEL), lambda i: (i, 0)),
            pl.BlockSpec((tm, D_MODEL), lambda i: (i, 1)),
            pl.BlockSpec((WIDTH, D_MODEL), lambda i: (0, 0), **resident),
            pl.BlockSpec((WIDTH, D_MODEL), lambda i: (0, 0), **resident),
            pl.BlockSpec((D_MODEL, D_MODEL), lambda i: (0, 0), **resident),
        ],
        out_specs=pl.BlockSpec((tm, D_MODEL), lambda i: (i, 0)),
        out_shape=jax.ShapeDtypeStruct((m, D_MODEL), F32),
        compiler_params=pltpu.CompilerParams(
            dimension_semantics=("parallel",), vmem_limit_bytes=VMEM_LIMIT),
        name="merge",
    )(x1, att_sb, att_ca, gates, gates, w_o_sb, w_o_ca, w_out)


def kernel(x, ffn1_norm, ffn1_w_gate, ffn1_w_up, ffn1_w_down, mix_norm, w_in, b_gate,
           q_norm_ca, k_norm_ca, rel_bias, w_o_sb, w_o_ca, w_out, ffn2_norm,
           ffn2_w_gate, ffn2_w_up, ffn2_w_down, final_norm):
    b, s, d = x.shape
    depth = ffn1_norm.shape[0]
    bf = lambda w: w.astype(BF16)
    row = lambda v: v.reshape(1, -1)
    xf = x.reshape(b * s, d)
    for l in range(depth):
        x1, hn = _ffn(xf, row(ffn1_norm[l]), ffn1_w_gate[l], ffn1_w_up[l], ffn1_w_down[l],
                      row(mix_norm[l]), emit_residual=True)
        qkv, (wg2,) = _inproj(hn, w_in[l], (COL_Q_SB, COL_K_SB, COL_V_SB, COL_V_CA), "plain",
                              side_casts=(ffn2_w_gate[l],))
        qk_ca, (wd2, wo_sb, wo_ca, wo) = _inproj(
            hn, w_in[l], (COL_Q_CA, COL_K_CA), "head_norm", jnp.stack([q_norm_ca[l], k_norm_ca[l]]),
            side_casts=(ffn2_w_down[l], w_o_sb[l], w_o_ca[l], w_out[l]))
        gates, (wu2,) = _inproj(hn, w_in[l], tuple(range(COL_GATE, IN_COLS // WIDTH)), "gate", row(b_gate[l]),
                                side_casts=(ffn2_w_up[l],))
        qkv3 = qkv.reshape(b, s, -1)
        att_sb = _sb_attention(qkv3, 0, 1, 2).reshape(b * s, WIDTH)
        att_ca = _ca_attention(qk_ca.reshape(b, s, -1), qkv3, 3, _ca_bias_row(rel_bias[l])).reshape(b * s, WIDTH)
        x2 = _merge(x1, att_sb, att_ca, gates, wo_sb, wo_ca, wo)
        (xf,) = _ffn(x2, row(ffn2_norm[l]), wg2, wu2, wd2, row(final_norm[l]), emit_residual=False)
    return xf.reshape(b, s, d)
```

```python
import functools

import jax
import jax.numpy as jnp
from jax import lax
from jax.experimental import pallas as pl
from jax.experimental.pallas import tpu as pltpu

D_MODEL = 2048
D_FF = 5632
HEAD_DIM = 128
N_HEADS = 8
WIDTH = N_HEADS * HEAD_DIM
CHUNK = 64
N_PAST_CHUNKS = 8
REL_CLIP_PAST = 128
N_REL = REL_CLIP_PAST + CHUNK
N_BRANCH = 2
IN_COLS = 6 * WIDTH + N_BRANCH * D_MODEL
EPS = 1e-6
SCALE = HEAD_DIM ** -0.5

COL_Q_SB, COL_K_SB, COL_V_SB, COL_Q_CA, COL_K_CA, COL_V_CA, COL_GATE = range(7)

V7X_VMEM_BYTES = 64 * 1024 * 1024
VMEM_LIMIT = V7X_VMEM_BYTES - 8 * 1024 * 1024

LOG_F32_UNDERFLOW = -104.0
LOG2_E = 1.4426950408889634
MASK_BIAS = -1e30

F32 = jnp.float32
BF16 = jnp.bfloat16
BF16_SUBLANES = 16


def _dot(a, b):
    return jnp.dot(a, b, preferred_element_type=F32)


def _dot_nt(a, b):
    return lax.dot_general(a, b, (((1,), (1,)), ((), ())), preferred_element_type=F32)


def _rmsnorm_f32(x, gain):
    return x * lax.rsqrt(jnp.mean(x * x, axis=-1, keepdims=True) + EPS) * gain


def _ffn_kernel(x_ref, nin_ref, wg_ref, wu_ref, wd_ref, nout_ref, *rest, emit_residual):
    if emit_residual:
        o_ref, hn_out_ref, hn_scr = rest
    else:
        o_ref, hn_scr = rest
    f = pl.program_id(1)

    @pl.when(f == 0)
    def _():
        x = x_ref[...]
        hn_scr[...] = _rmsnorm_f32(x, nin_ref[...]).astype(BF16)
        o_ref[...] = x

    h = hn_scr[...]
    g = _dot(h, wg_ref[...])
    u = _dot(h, wu_ref[...])
    a = (g * u * 0.5) / (1.0 + jnp.exp(-g))
    o_ref[...] += _dot(a.astype(BF16), wd_ref[...])

    @pl.when(f == pl.num_programs(1) - 1)
    def _():
        y = _rmsnorm_f32(o_ref[...], nout_ref[...])
        if emit_residual:
            hn_out_ref[...] = y.astype(BF16)
        else:
            o_ref[...] = y


def _ffn(x, norm_in, wg, wu, wd, norm_out, *, emit_residual, tm=512, tf=512):
    m = x.shape[0]
    grid = (m // tm, D_FF // tf)
    wg, wu, wd = wg.astype(BF16), wu.astype(BF16), wd.astype(BF16)
    out_shape = [jax.ShapeDtypeStruct((m, D_MODEL), F32)]
    out_specs = [pl.BlockSpec((tm, D_MODEL), lambda i, f: (i, 0))]
    if emit_residual:
        out_shape.append(jax.ShapeDtypeStruct((m, D_MODEL), BF16))
        out_specs.append(pl.BlockSpec((tm, D_MODEL), lambda i, f: (i, 0)))
    res = pl.pallas_call(
        functools.partial(_ffn_kernel, emit_residual=emit_residual),
        grid=grid,
        in_specs=[
            pl.BlockSpec((tm, D_MODEL), lambda i, f: (i, 0)),
            pl.BlockSpec((1, D_MODEL), lambda i, f: (0, 0)),
            pl.BlockSpec((D_MODEL, tf), lambda i, f: (0, f)),
            pl.BlockSpec((D_MODEL, tf), lambda i, f: (0, f)),
            pl.BlockSpec((tf, D_MODEL), lambda i, f: (f, 0)),
            pl.BlockSpec((1, D_MODEL), lambda i, f: (0, 0)),
        ],
        out_specs=out_specs,
        out_shape=out_shape,
        scratch_shapes=[pltpu.VMEM((tm, D_MODEL), BF16)],
        compiler_params=pltpu.CompilerParams(
            dimension_semantics=("parallel", "arbitrary"), vmem_limit_bytes=VMEM_LIMIT),
        name="ffn_res" if emit_residual else "ffn_final",
    )(x, norm_in, wg, wu, wd, norm_out)
    return res


def _inproj_kernel(h_ref, w_ref, *rest, epilogue, n_side):
    rest = list(rest)
    aux_ref = None if epilogue == "plain" else rest.pop(0)
    side_in = [rest.pop(0) for _ in range(n_side)]
    o_ref = rest.pop(0)
    side_out = [rest.pop(0) for _ in range(n_side)]
    (w_scr,) = rest

    @pl.when(pl.program_id(1) == 0)
    def _():
        w_scr[...] = w_ref[...].astype(BF16)

    for src, dst in zip(side_in, side_out):
        dst[...] = src[...].astype(BF16)

    r = _dot(h_ref[...], w_scr[...])
    if epilogue == "plain":
        o_ref[...] = r.astype(BF16)
    elif epilogue == "head_norm":
        gains = aux_ref[pl.ds(pl.program_id(0), 1), :]
        for hd in range(N_HEADS):
            sl = slice(hd * HEAD_DIM, (hd + 1) * HEAD_DIM)
            o_ref[:, sl] = _rmsnorm_f32(r[:, sl], gains).astype(BF16)
    else:
        assert epilogue == "gate"
        o_ref[...] = (1.0 / (1.0 + jnp.exp(-(r + aux_ref[...])))).astype(BF16)


def _inproj(hn, w_in, w_col_blocks, epilogue, aux=None, *, side_casts=(), tm=1024):
    m = hn.shape[0]
    n = len(w_col_blocks)
    row_tiles = m // tm
    steps = n * row_tiles
    first, last = w_col_blocks[0], w_col_blocks[-1]
    assert list(w_col_blocks[:-1]) == list(range(first, first + n - 1)) and last >= first + n - 1
    assert (aux is None) == (epilogue == "plain")
    aux_specs = {
        "plain": [],
        "gate": [pl.BlockSpec((1, WIDTH), lambda j, i: (0, j))],
        "head_norm": [pl.BlockSpec((n, HEAD_DIM), lambda j, i: (0, 0))],
    }[epilogue]
    side_specs = []
    for a in side_casts:
        rows, cols = a.shape
        slab = rows // steps
        assert slab * steps == rows and slab % BF16_SUBLANES == 0, (a.shape, steps)
        side_specs.append(pl.BlockSpec((slab, cols), lambda j, i: (j * row_tiles + i, 0)))
    res = pl.pallas_call(
        functools.partial(_inproj_kernel, epilogue=epilogue, n_side=len(side_casts)),
        grid=(n, row_tiles),
        in_specs=[
            pl.BlockSpec((tm, D_MODEL), lambda j, i: (i, 0)),
            pl.BlockSpec((D_MODEL, WIDTH), lambda j, i: (0, first + j + (last - first - n + 1) * (j // (n - 1)))),
            *aux_specs,
            *side_specs,
        ],
        out_specs=[pl.BlockSpec((tm, WIDTH), lambda j, i: (i, j)), *side_specs],
        out_shape=[jax.ShapeDtypeStruct((m, n * WIDTH), BF16),
                   *[jax.ShapeDtypeStruct(a.shape, BF16) for a in side_casts]],
        scratch_shapes=[pltpu.VMEM((D_MODEL, WIDTH), BF16)],
        compiler_params=pltpu.CompilerParams(
            dimension_semantics=("arbitrary", "arbitrary"), vmem_limit_bytes=VMEM_LIMIT),
        name="inproj_" + epilogue,
    )(hn, w_in, *([] if aux is None else [aux]), *side_casts)
    return res[0], res[1:]


SB_BLK = 256
SB_HEADS = 4


def _sb_kernel(q_ref, k_ref, v_ref, o_ref):
    blk = SB_BLK
    row = lax.broadcasted_iota(jnp.int32, (blk, blk), 0)
    col = lax.broadcasted_iota(jnp.int32, (blk, blk), 1)
    later_sum = (row > col).astype(BF16)
    causal = col < row

    def sweep_block(qs, kb, state, diagonal):
        start = pl.multiple_of(kb * blk, blk)
        log_keeps, log_betas, splits = [], [], []
        for hd in range(SB_HEADS):
            lanes = slice(hd * HEAD_DIM, (hd + 1) * HEAD_DIM)
            nz = _dot_nt(qs[hd], k_ref[pl.ds(start, blk), lanes]) * (-SCALE * LOG2_E)
            log_keep = jnp.minimum(nz, 0.0) - jnp.log2(1.0 + jnp.exp2(-jnp.abs(nz)))
            log_betas.append(log_keep - nz)
            if diagonal:
                log_keep = jnp.where(causal, log_keep, 0.0)
            log_keeps.append(log_keep)
            hi = log_keep.astype(BF16)
            splits += [hi, (log_keep - hi.astype(F32)).astype(BF16)]
        later_all = _dot(jnp.concatenate(splits, axis=0), later_sum)
        new_state = []
        for hd in range(SB_HEADS):
            run, acc = state[hd]
            lanes = slice(hd * HEAD_DIM, (hd + 1) * HEAD_DIM)
            later = later_all[2 * hd * blk:(2 * hd + 1) * blk] + later_all[(2 * hd + 1) * blk:(2 * hd + 2) * blk]
            w = jnp.exp2(log_betas[hd] + later + run)
            if diagonal:
                w = jnp.where(causal, w, 0.0)
            acc = acc + _dot(w.astype(BF16), v_ref[pl.ds(start, blk), lanes])
            run = run + jnp.sum(log_keeps[hd], axis=1, keepdims=True)
            new_state.append((run, acc))
        return tuple(new_state)

    def query_tile(qi, first):
        q0 = pl.multiple_of(qi * blk, blk)
        qs = [q_ref[pl.ds(q0, blk), hd * HEAD_DIM:(hd + 1) * HEAD_DIM] for hd in range(SB_HEADS)]
        run0 = jnp.zeros((blk, 1), F32)
        acc0 = jnp.zeros((blk, HEAD_DIM), F32)
        state = sweep_block(qs, qi, ((run0, acc0),) * SB_HEADS, True)
        if not first:
            state = sweep_block(qs, qi - 1, state, False)

            def cond(c):
                kb, st = c
                worst = functools.reduce(jnp.maximum, [run for run, _ in st])
                return jnp.logical_and(kb >= 0, jnp.max(worst) > LOG_F32_UNDERFLOW * LOG2_E)

            def body(c):
                kb, st = c
                return kb - 1, sweep_block(qs, kb, st, False)

            _, state = lax.while_loop(cond, body, (qi - 2, state))
        for hd in range(SB_HEADS):
            o_ref[pl.ds(q0, blk), hd * HEAD_DIM:(hd + 1) * HEAD_DIM] = state[hd][1].astype(BF16)

    query_tile(jnp.int32(0), True)
    lax.fori_loop(1, q_ref.shape[0] // blk, lambda qi, c: (query_tile(qi, False), c)[1], 0)


def _sb_attention(qkv, q_col, k_col, v_col):
    b, s, _ = qkv.shape
    groups = N_HEADS // SB_HEADS
    blk = (None, s, SB_HEADS * HEAD_DIM)

    def col(c):
        return lambda bi, g: (bi, 0, c * groups + g)

    return pl.pallas_call(
        _sb_kernel,
        grid=(b, groups),
        in_specs=[pl.BlockSpec(blk, col(q_col)), pl.BlockSpec(blk, col(k_col)), pl.BlockSpec(blk, col(v_col))],
        out_specs=pl.BlockSpec(blk, col(0)),
        out_shape=jax.ShapeDtypeStruct((b, s, WIDTH), BF16),
        compiler_params=pltpu.CompilerParams(
            dimension_semantics=("parallel", "parallel"), vmem_limit_bytes=VMEM_LIMIT),
        name="sb_attn",
    )(qkv, qkv, qkv)


CA_TQ = 128
CA_PAST = N_PAST_CHUNKS * CHUNK
CA_WIN = CA_PAST + CA_TQ
CA_ROLL = 1024
CA_GROUP = 4


def _ca_bias_row(rel_bias):
    far = N_PAST_CHUNKS * CHUNK - REL_CLIP_PAST + 1
    past = jnp.broadcast_to(rel_bias[:, N_REL - 1:], (N_HEADS, far))
    ramp = rel_bias[:, N_REL - 2::-1]
    future = jnp.broadcast_to(rel_bias[:, :1], (N_HEADS, CA_WIN + 1 - far - (N_REL - 1)))
    wrap = jnp.broadcast_to(rel_bias[:, N_REL - 1:], (N_HEADS, CA_ROLL - CA_WIN - 1))
    return jnp.concatenate([past, ramp, future, wrap], axis=1).astype(F32)[:, None, :]


def _ca_kernel(q_ref, k_ref, v_ref, brow_ref, o_ref, bias_scr):
    @pl.when(pl.program_id(1) == 0)
    def _():
        rows = jnp.broadcast_to(brow_ref[...], (CA_TQ, CA_ROLL))
        table = pltpu.roll(rows, 0, 1, stride=1, stride_axis=0)[:, :CA_WIN]
        q_chunk = lax.broadcasted_iota(jnp.int32, (CA_TQ, CA_WIN), 0) // CHUNK
        k_chunk = lax.broadcasted_iota(jnp.int32, (CA_TQ, CA_WIN), 1) // CHUNK
        in_band = (k_chunk >= q_chunk) & (k_chunk <= q_chunk + N_PAST_CHUNKS)
        bias_scr[...] = jnp.where(in_band, table, MASK_BIAS)

    def query_tiles(tiles):
        scores = [_dot_nt(q_ref[pl.ds(q0, CA_TQ), :], k_ref[pl.ds(k0, n), :]) * SCALE + bias_scr[:, CA_WIN - n:]
                  for q0, k0, n in tiles]
        probs = [jnp.exp(s - jnp.max(s, axis=1, keepdims=True)) for s in scores]
        outs = [_dot(p.astype(BF16), v_ref[pl.ds(k0, n), :]) for p, (_, k0, n) in zip(probs, tiles)]
        for p, o, (q0, _, _) in zip(probs, outs, tiles):
            o_ref[pl.ds(q0, CA_TQ), :] = (o / jnp.sum(p, axis=1, keepdims=True)).astype(BF16)

    n_head = CA_PAST // CA_TQ
    query_tiles([(t * CA_TQ, 0, (t + 1) * CA_TQ) for t in range(n_head)])

    def full_tiles(g, carry):
        q0 = pl.multiple_of((n_head + g * CA_GROUP) * CA_TQ, CA_TQ)
        query_tiles([(q0 + j * CA_TQ, q0 + j * CA_TQ - CA_PAST, CA_WIN) for j in range(CA_GROUP)])
        return carry

    n_tiles = q_ref.shape[0] // CA_TQ
    assert (n_tiles - n_head) % CA_GROUP == 0
    lax.fori_loop(0, (n_tiles - n_head) // CA_GROUP, full_tiles, 0)


def _ca_attention(qk, v, v_col, bias_row):
    b, s, _ = qk.shape
    blk = (None, s, HEAD_DIM)

    def col(c):
        return lambda h, bi: (bi, 0, c * N_HEADS + h)

    return pl.pallas_call(
        _ca_kernel,
        grid=(N_HEADS, b),
        in_specs=[
            pl.BlockSpec(blk, col(0)),
            pl.BlockSpec(blk, col(1)),
            pl.BlockSpec(blk, col(v_col)),
            pl.BlockSpec((None, 1, CA_ROLL), lambda h, bi: (h, 0, 0)),
        ],
        out_specs=pl.BlockSpec(blk, lambda h, bi: (bi, 0, h)),
        out_shape=jax.ShapeDtypeStruct((b, s, WIDTH), BF16),
        scratch_shapes=[pltpu.VMEM((CA_TQ, CA_WIN), F32)],
        compiler_params=pltpu.CompilerParams(
            dimension_semantics=("arbitrary", "arbitrary"), vmem_limit_bytes=VMEM_LIMIT),
        name="ca_attn",
    )(qk, qk, v, bias_row)


def _merge_kernel(x_ref, a_ref, c_ref, ga_ref, gb_ref, woa_ref, wob_ref, wout_ref, o_ref):
    ya = _dot(a_ref[...], woa_ref[...])
    yb = _dot(c_ref[...], wob_ref[...])
    merged = ga_ref[...].astype(F32) * ya + gb_ref[...].astype(F32) * yb
    o_ref[...] = x_ref[...] + _dot(merged.astype(BF16), wout_ref[...])


def _merge(x1, att_sb, att_ca, gates, w_o_sb, w_o_ca, w_out, *, tm=512):
    m = x1.shape[0]
    resident = dict(pipeline_mode=pl.Buffered(1))
    return pl.pallas_call(
        _merge_kernel,
        grid=(m // tm,),
        in_specs=[
            pl.BlockSpec((tm, D_MODEL), lambda i: (i, 0)),
            pl.BlockSpec((tm, WIDTH), lambda i: (i, 0)),
            pl.BlockSpec((tm, WIDTH), lambda i: (i, 0)),
            pl.BlockSpec((tm, D_MODEL), lambda i: (i, 0)),
            pl.BlockSpec((tm, D_MODEL), lambda i: (i, 1)),
            pl.BlockSpec((WIDTH, D_MODEL), lambda i: (0, 0), **resident),
            pl.BlockSpec((WIDTH, D_MODEL), lambda i: (0, 0), **resident),
            pl.BlockSpec((D_MODEL, D_MODEL), lambda i: (0, 0), **resident),
        ],
        out_specs=pl.BlockSpec((tm, D_MODEL), lambda i: (i, 0)),
        out_shape=jax.ShapeDtypeStruct((m, D_MODEL), F32),
        compiler_params=pltpu.CompilerParams(
            dimension_semantics=("parallel",), vmem_limit_bytes=VMEM_LIMIT),
        name="merge",
    )(x1, att_sb, att_ca, gates, gates, w_o_sb, w_o_ca, w_out)


def kernel(x, ffn1_norm, ffn1_w_gate, ffn1_w_up, ffn1_w_down, mix_norm, w_in, b_gate,
           q_norm_ca, k_norm_ca, rel_bias, w_o_sb, w_o_ca, w_out, ffn2_norm,
           ffn2_w_gate, ffn2_w_up, ffn2_w_down, final_norm):
    b, s, d = x.shape
    depth = ffn1_norm.shape[0]
    bf = lambda w: w.astype(BF16)
    row = lambda v: v.reshape(1, -1)
    xf = x.reshape(b * s, d)
    for l in range(depth):
        x1, hn = _ffn(xf, row(ffn1_norm[l]), ffn1_w_gate[l], ffn1_w_up[l], ffn1_w_down[l],
                      row(mix_norm[l]), emit_residual=True)
        qkv, (wg2,) = _inproj(hn, w_in[l], (COL_Q_SB, COL_K_SB, COL_V_SB, COL_V_CA), "plain",
                              side_casts=(ffn2_w_gate[l],))
        qk_ca, (wd2, wo_sb, wo_ca, wo) = _inproj(
            hn, w_in[l], (COL_Q_CA, COL_K_CA), "head_norm", jnp.stack([q_norm_ca[l], k_norm_ca[l]]),
            side_casts=(ffn2_w_down[l], w_o_sb[l], w_o_ca[l], w_out[l]))
        gates, (wu2,) = _inproj(hn, w_in[l], tuple(range(COL_GATE, IN_COLS // WIDTH)), "gate", row(b_gate[l]),
                                side_casts=(ffn2_w_up[l],))
        qkv3 = qkv.reshape(b, s, -1)
        att_sb = _sb_attention(qkv3, 0, 1, 2).reshape(b * s, WIDTH)
        att_ca = _ca_attention(qk_ca.reshape(b, s, -1), qkv3, 3, _ca_bias_row(rel_bias[l])).reshape(b * s, WIDTH)
        x2 = _merge(x1, att_sb, att_ca, gates, wo_sb, wo_ca, wo)
        (xf,) = _ffn(x2, row(ffn2_norm[l]), wg2, wu2, wd2, row(final_norm[l]), emit_residual=False)
    return xf.reshape(b, s, d)
```

```python
import functools

import jax
import jax.numpy as jnp
from jax import lax
from jax.experimental import pallas as pl
from jax.experimental.pallas import tpu as pltpu

D_MODEL = 2048
D_FF = 5632
HEAD_DIM = 128
N_HEADS = 8
WIDTH = N_HEADS * HEAD_DIM
CHUNK = 64
N_PAST_CHUNKS = 8
REL_CLIP_PAST = 128
N_REL = REL_CLIP_PAST + CHUNK
N_BRANCH = 2
IN_COLS = 6 * WIDTH + N_BRANCH * D_MODEL
EPS = 1e-6
SCALE = HEAD_DIM ** -0.5

COL_Q_SB, COL_K_SB, COL_V_SB, COL_Q_CA, COL_K_CA, COL_V_CA, COL_GATE = range(7)

V7X_VMEM_BYTES = 64 * 1024 * 1024
VMEM_LIMIT = V7X_VMEM_BYTES - 8 * 1024 * 1024

LOG_F32_UNDERFLOW = -104.0
LOG2_E = 1.4426950408889634
MASK_BIAS = -1e30

F32 = jnp.float32
BF16 = jnp.bfloat16
BF16_SUBLANES = 16


def _dot(a, b):
    return jnp.dot(a, b, preferred_element_type=F32)


def _dot_nt(a, b):
    return lax.dot_general(a, b, (((1,), (1,)), ((), ())), preferred_element_type=F32)


def _rmsnorm_f32(x, gain):
    return x * lax.rsqrt(jnp.mean(x * x, axis=-1, keepdims=True) + EPS) * gain


def _ffn_kernel(x_ref, nin_ref, wg_ref, wu_ref, wd_ref, nout_ref, *rest, emit_residual):
    if emit_residual:
        o_ref, hn_out_ref, hn_scr = rest
    else:
        o_ref, hn_scr = rest
    f = pl.program_id(1)

    @pl.when(f == 0)
    def _():
        x = x_ref[...]
        hn_scr[...] = _rmsnorm_f32(x, nin_ref[...]).astype(BF16)
        o_ref[...] = x

    h = hn_scr[...]
    g = _dot(h, wg_ref[...])
    u = _dot(h, wu_ref[...])
    a = (g * u * 0.5) / (1.0 + jnp.exp(-g))
    o_ref[...] += _dot(a.astype(BF16), wd_ref[...].astype(BF16))

    @pl.when(f == pl.num_programs(1) - 1)
    def _():
        y = _rmsnorm_f32(o_ref[...], nout_ref[...])
        if emit_residual:
            hn_out_ref[...] = y.astype(BF16)
        else:
            o_ref[...] = y


def _ffn(x, norm_in, wg, wu, wd, norm_out, *, emit_residual, tm=512, tf=512):
    m = x.shape[0]
    grid = (m // tm, D_FF // tf)
    wg, wu = wg.astype(BF16), wu.astype(BF16)
    out_shape = [jax.ShapeDtypeStruct((m, D_MODEL), F32)]
    out_specs = [pl.BlockSpec((tm, D_MODEL), lambda i, f: (i, 0))]
    if emit_residual:
        out_shape.append(jax.ShapeDtypeStruct((m, D_MODEL), BF16))
        out_specs.append(pl.BlockSpec((tm, D_MODEL), lambda i, f: (i, 0)))
    res = pl.pallas_call(
        functools.partial(_ffn_kernel, emit_residual=emit_residual),
        grid=grid,
        in_specs=[
            pl.BlockSpec((tm, D_MODEL), lambda i, f: (i, 0)),
            pl.BlockSpec((1, D_MODEL), lambda i, f: (0, 0)),
            pl.BlockSpec((D_MODEL, tf), lambda i, f: (0, f)),
            pl.BlockSpec((D_MODEL, tf), lambda i, f: (0, f)),
            pl.BlockSpec((tf, D_MODEL), lambda i, f: (f, 0)),
            pl.BlockSpec((1, D_MODEL), lambda i, f: (0, 0)),
        ],
        out_specs=out_specs,
        out_shape=out_shape,
        scratch_shapes=[pltpu.VMEM((tm, D_MODEL), BF16)],
        compiler_params=pltpu.CompilerParams(
            dimension_semantics=("parallel", "arbitrary"), vmem_limit_bytes=VMEM_LIMIT),
        name="ffn_res" if emit_residual else "ffn_final",
    )(x, norm_in, wg, wu, wd, norm_out)
    return res


def _inproj_kernel(h_ref, w_ref, *rest, epilogue, n_side):
    rest = list(rest)
    aux_ref = None if epilogue == "plain" else rest.pop(0)
    side_in = [rest.pop(0) for _ in range(n_side)]
    o_ref = rest.pop(0)
    side_out = [rest.pop(0) for _ in range(n_side)]
    (w_scr,) = rest

    @pl.when(pl.program_id(1) == 0)
    def _():
        w_scr[...] = w_ref[...].astype(BF16)

    for src, dst in zip(side_in, side_out):
        dst[...] = src[...].astype(BF16)

    r = _dot(h_ref[...], w_scr[...])
    if epilogue == "plain":
        o_ref[...] = r.astype(BF16)
    elif epilogue == "head_norm":
        gains = aux_ref[pl.ds(pl.program_id(0), 1), :]
        for hd in range(N_HEADS):
            sl = slice(hd * HEAD_DIM, (hd + 1) * HEAD_DIM)
            o_ref[:, sl] = _rmsnorm_f32(r[:, sl], gains).astype(BF16)
    else:
        assert epilogue == "gate"
        o_ref[...] = (1.0 / (1.0 + jnp.exp(-(r + aux_ref[...])))).astype(BF16)


def _inproj(hn, w_in, w_col_blocks, epilogue, aux=None, *, side_casts=(), tm=1024):
    m = hn.shape[0]
    n = len(w_col_blocks)
    row_tiles = m // tm
    steps = n * row_tiles
    first, last = w_col_blocks[0], w_col_blocks[-1]
    assert list(w_col_blocks[:-1]) == list(range(first, first + n - 1)) and last >= first + n - 1
    assert (aux is None) == (epilogue == "plain")
    aux_specs = {
        "plain": [],
        "gate": [pl.BlockSpec((1, WIDTH), lambda j, i: (0, j))],
        "head_norm": [pl.BlockSpec((n, HEAD_DIM), lambda j, i: (0, 0))],
    }[epilogue]
    side_specs = []
    for a in side_casts:
        rows, cols = a.shape
        slab = rows // steps
        assert slab * steps == rows and slab % BF16_SUBLANES == 0, (a.shape, steps)
        side_specs.append(pl.BlockSpec((slab, cols), lambda j, i: (j * row_tiles + i, 0)))
    res = pl.pallas_call(
        functools.partial(_inproj_kernel, epilogue=epilogue, n_side=len(side_casts)),
        grid=(n, row_tiles),
        in_specs=[
            pl.BlockSpec((tm, D_MODEL), lambda j, i: (i, 0)),
            pl.BlockSpec((D_MODEL, WIDTH), lambda j, i: (0, first + j + (last - first - n + 1) * (j // (n - 1)))),
            *aux_specs,
            *side_specs,
        ],
        out_specs=[pl.BlockSpec((tm, WIDTH), lambda j, i: (i, j)), *side_specs],
        out_shape=[jax.ShapeDtypeStruct((m, n * WIDTH), BF16),
                   *[jax.ShapeDtypeStruct(a.shape, BF16) for a in side_casts]],
        scratch_shapes=[pltpu.VMEM((D_MODEL, WIDTH), BF16)],
        compiler_params=pltpu.CompilerParams(
            dimension_semantics=("arbitrary", "arbitrary"), vmem_limit_bytes=VMEM_LIMIT),
        name="inproj_" + epilogue,
    )(hn, w_in, *([] if aux is None else [aux]), *side_casts)
    return res[0], res[1:]


SB_BLK = 256
SB_HEADS = 4


def _sb_kernel(q_ref, k_ref, v_ref, o_ref):
    blk = SB_BLK
    row = lax.broadcasted_iota(jnp.int32, (blk, blk), 0)
    col = lax.broadcasted_iota(jnp.int32, (blk, blk), 1)
    later_sum = (row > col).astype(BF16)
    causal = col < row

    def sweep_block(qs, kb, state, diagonal):
        start = pl.multiple_of(kb * blk, blk)
        log_keeps, log_betas, splits = [], [], []
        for hd in range(SB_HEADS):
            lanes = slice(hd * HEAD_DIM, (hd + 1) * HEAD_DIM)
            nz = _dot_nt(qs[hd], k_ref[pl.ds(start, blk), lanes]) * (-SCALE * LOG2_E)
            log_keep = jnp.minimum(nz, 0.0) - jnp.log2(1.0 + jnp.exp2(-jnp.abs(nz)))
            log_betas.append(log_keep - nz)
            if diagonal:
                log_keep = jnp.where(causal, log_keep, 0.0)
            log_keeps.append(log_keep)
            hi = log_keep.astype(BF16)
            splits += [hi, (log_keep - hi.astype(F32)).astype(BF16)]
        later_all = _dot(jnp.concatenate(splits, axis=0), later_sum)
        new_state = []
        for hd in range(SB_HEADS):
            run, acc = state[hd]
            lanes = slice(hd * HEAD_DIM, (hd + 1) * HEAD_DIM)
            later = later_all[2 * hd * blk:(2 * hd + 1) * blk] + later_all[(2 * hd + 1) * blk:(2 * hd + 2) * blk]
            w = jnp.exp2(log_betas[hd] + later + run)
            if diagonal:
                w = jnp.where(causal, w, 0.0)
            acc = acc + _dot(w.astype(BF16), v_ref[pl.ds(start, blk), lanes])
            run = run + jnp.sum(log_keeps[hd], axis=1, keepdims=True)
            new_state.append((run, acc))
        return tuple(new_state)

    def query_tile(qi, first):
        q0 = pl.multiple_of(qi * blk, blk)
        qs = [q_ref[pl.ds(q0, blk), hd * HEAD_DIM:(hd + 1) * HEAD_DIM] for hd in range(SB_HEADS)]
        run0 = jnp.zeros((blk, 1), F32)
        acc0 = jnp.zeros((blk, HEAD_DIM), F32)
        state = sweep_block(qs, qi, ((run0, acc0),) * SB_HEADS, True)
        if not first:
            state = sweep_block(qs, qi - 1, state, False)

            def cond(c):
                kb, st = c
                worst = functools.reduce(jnp.maximum, [run for run, _ in st])
                return jnp.logical_and(kb >= 0, jnp.max(worst) > LOG_F32_UNDERFLOW * LOG2_E)

            def body(c):
                kb, st = c
                return kb - 1, sweep_block(qs, kb, st, False)

            _, state = lax.while_loop(cond, body, (qi - 2, state))
        for hd in range(SB_HEADS):
            o_ref[pl.ds(q0, blk), hd * HEAD_DIM:(hd + 1) * HEAD_DIM] = state[hd][1].astype(BF16)

    query_tile(jnp.int32(0), True)
    lax.fori_loop(1, q_ref.shape[0] // blk, lambda qi, c: (query_tile(qi, False), c)[1], 0)


def _sb_attention(qkv, q_col, k_col, v_col):
    b, s, _ = qkv.shape
    groups = N_HEADS // SB_HEADS
    blk = (None, s, SB_HEADS * HEAD_DIM)

    def col(c):
        return lambda bi, g: (bi, 0, c * groups + g)

    return pl.pallas_call(
        _sb_kernel,
        grid=(b, groups),
        in_specs=[pl.BlockSpec(blk, col(q_col)), pl.BlockSpec(blk, col(k_col)), pl.BlockSpec(blk, col(v_col))],
        out_specs=pl.BlockSpec(blk, col(0)),
        out_shape=jax.ShapeDtypeStruct((b, s, WIDTH), BF16),
        compiler_params=pltpu.CompilerParams(
            dimension_semantics=("parallel", "parallel"), vmem_limit_bytes=VMEM_LIMIT),
        name="sb_attn",
    )(qkv, qkv, qkv)


CA_TQ = 128
CA_PAST = N_PAST_CHUNKS * CHUNK
CA_WIN = CA_PAST + CA_TQ
CA_ROLL = 1024
CA_GROUP = 4


def _ca_bias_row(rel_bias):
    far = N_PAST_CHUNKS * CHUNK - REL_CLIP_PAST + 1
    past = jnp.broadcast_to(rel_bias[:, N_REL - 1:], (N_HEADS, far))
    ramp = rel_bias[:, N_REL - 2::-1]
    future = jnp.broadcast_to(rel_bias[:, :1], (N_HEADS, CA_WIN + 1 - far - (N_REL - 1)))
    wrap = jnp.broadcast_to(rel_bias[:, N_REL - 1:], (N_HEADS, CA_ROLL - CA_WIN - 1))
    return jnp.concatenate([past, ramp, future, wrap], axis=1).astype(F32)[:, None, :]


def _ca_kernel(q_ref, k_ref, v_ref, brow_ref, o_ref, bias_scr):
    @pl.when(pl.program_id(1) == 0)
    def _():
        rows = jnp.broadcast_to(brow_ref[...], (CA_TQ, CA_ROLL))
        table = pltpu.roll(rows, 0, 1, stride=1, stride_axis=0)[:, :CA_WIN]
        q_chunk = lax.broadcasted_iota(jnp.int32, (CA_TQ, CA_WIN), 0) // CHUNK
        k_chunk = lax.broadcasted_iota(jnp.int32, (CA_TQ, CA_WIN), 1) // CHUNK
        in_band = (k_chunk >= q_chunk) & (k_chunk <= q_chunk + N_PAST_CHUNKS)
        bias_scr[...] = jnp.where(in_band, table, MASK_BIAS)

    def query_tiles(tiles):
        scores = [_dot_nt(q_ref[pl.ds(q0, CA_TQ), :], k_ref[pl.ds(k0, n), :]) * SCALE + bias_scr[:, CA_WIN - n:]
                  for q0, k0, n in tiles]
        probs = [jnp.exp(s - jnp.max(s, axis=1, keepdims=True)) for s in scores]
        outs = [_dot(p.astype(BF16), v_ref[pl.ds(k0, n), :]) for p, (_, k0, n) in zip(probs, tiles)]
        for p, o, (q0, _, _) in zip(probs, outs, tiles):
            o_ref[pl.ds(q0, CA_TQ), :] = (o / jnp.sum(p, axis=1, keepdims=True)).astype(BF16)

    n_head = CA_PAST // CA_TQ
    query_tiles([(t * CA_TQ, 0, (t + 1) * CA_TQ) for t in range(n_head)])

    def full_tiles(g, carry):
        q0 = pl.multiple_of((n_head + g * CA_GROUP) * CA_TQ, CA_TQ)
        query_tiles([(q0 + j * CA_TQ, q0 + j * CA_TQ - CA_PAST, CA_WIN) for j in range(CA_GROUP)])
        return carry

    n_tiles = q_ref.shape[0] // CA_TQ
    assert (n_tiles - n_head) % CA_GROUP == 0
    lax.fori_loop(0, (n_tiles - n_head) // CA_GROUP, full_tiles, 0)


def _ca_attention(qk, v, v_col, bias_row):
    b, s, _ = qk.shape
    blk = (None, s, HEAD_DIM)

    def col(c):
        return lambda h, bi: (bi, 0, c * N_HEADS + h)

    return pl.pallas_call(
        _ca_kernel,
        grid=(N_HEADS, b),
        in_specs=[
            pl.BlockSpec(blk, col(0)),
            pl.BlockSpec(blk, col(1)),
            pl.BlockSpec(blk, col(v_col)),
            pl.BlockSpec((None, 1, CA_ROLL), lambda h, bi: (h, 0, 0)),
        ],
        out_specs=pl.BlockSpec(blk, lambda h, bi: (bi, 0, h)),
        out_shape=jax.ShapeDtypeStruct((b, s, WIDTH), BF16),
        scratch_shapes=[pltpu.VMEM((CA_TQ, CA_WIN), F32)],
        compiler_params=pltpu.CompilerParams(
            dimension_semantics=("arbitrary", "arbitrary"), vmem_limit_bytes=VMEM_LIMIT),
        name="ca_attn",
    )(qk, qk, v, bias_row)


def _merge_kernel(x_ref, a_ref, c_ref, ga_ref, gb_ref, woa_ref, wob_ref, wout_ref, o_ref):
    ya = _dot(a_ref[...], woa_ref[...])
    yb = _dot(c_ref[...], wob_ref[...])
    merged = ga_ref[...].astype(F32) * ya + gb_ref[...].astype(F32) * yb
    o_ref[...] = x_ref[...] + _dot(merged.astype(BF16), wout_ref[...])


def _merge(x1, att_sb, att_ca, gates, w_o_sb, w_o_ca, w_out, *, tm=512):
    m = x1.shape[0]
    resident = dict(pipeline_mode=pl.Buffered(1))
    return pl.pallas_call(
        _merge_kernel,
        grid=(m // tm,),
        in_specs=[
            pl.BlockSpec((tm, D_MODEL), lambda i: (i, 0)),
            pl.BlockSpec((tm, WIDTH), lambda i: (i, 0)),
            pl.BlockSpec((tm, WIDTH), lambda i: (i, 0)),
            pl.BlockSpec((tm, D_MODEL), lambda i: (i, 0)),
            pl.BlockSpec((tm, D_MODEL), lambda i: (i, 1)),
            pl.BlockSpec((WIDTH, D_MODEL), lambda i: (0, 0), **resident),
            pl.BlockSpec((WIDTH, D_MODEL), lambda i: (0, 0), **resident),
            pl.BlockSpec((D_MODEL, D_MODEL), lambda i: (0, 0), **resident),
        ],
        out_specs=pl.BlockSpec((tm, D_MODEL), lambda i: (i, 0)),
        out_shape=jax.ShapeDtypeStruct((m, D_MODEL), F32),
        compiler_params=pltpu.CompilerParams(
            dimension_semantics=("parallel",), vmem_limit_bytes=VMEM_LIMIT),
        name="merge",
    )(x1, att_sb, att_ca, gates, gates, w_o_sb, w_o_ca, w_out)


def kernel(x, ffn1_norm, ffn1_w_gate, ffn1_w_up, ffn1_w_down, mix_norm, w_in, b_gate,
           q_norm_ca, k_norm_ca, rel_bias, w_o_sb, w_o_ca, w_out, ffn2_norm,
           ffn2_w_gate, ffn2_w_up, ffn2_w_down, final_norm):
    b, s, d = x.shape
    depth = ffn1_norm.shape[0]
    row = lambda v: v.reshape(1, -1)
    xf = x.reshape(b * s, d)
    for l in range(depth):
        x1, hn = _ffn(xf, row(ffn1_norm[l]), ffn1_w_gate[l], ffn1_w_up[l], ffn1_w_down[l],
                      row(mix_norm[l]), emit_residual=True)
        qkv, (wg2,) = _inproj(hn, w_in[l], (COL_Q_SB, COL_K_SB, COL_V_SB, COL_V_CA), "plain",
                              side_casts=(ffn2_w_gate[l],))
        qk_ca, (wd2, wo_sb, wo_ca, wo) = _inproj(
            hn, w_in[l], (COL_Q_CA, COL_K_CA), "head_norm", jnp.stack([q_norm_ca[l], k_norm_ca[l]]),
            side_casts=(ffn2_w_down[l], w_o_sb[l], w_o_ca[l], w_out[l]))
        gates, (wu2,) = _inproj(hn, w_in[l], tuple(range(COL_GATE, IN_COLS // WIDTH)), "gate", row(b_gate[l]),
                                side_casts=(ffn2_w_up[l],))
        qkv3 = qkv.reshape(b, s, -1)
        att_sb = _sb_attention(qkv3, 0, 1, 2).reshape(b * s, WIDTH)
        att_ca = _ca_attention(qk_ca.reshape(b, s, -1), qkv3, 3, _ca_bias_row(rel_bias[l])).reshape(b * s, WIDTH)
        x2 = _merge(x1, att_sb, att_ca, gates, wo_sb, wo_ca, wo)
        (xf,) = _ffn(x2, row(ffn2_norm[l]), wg2, wu2, wd2, row(final_norm[l]), emit_residual=False)
    return xf.reshape(b, s, d)
```

```python
import functools

import jax
import jax.numpy as jnp
from jax import lax
from jax.experimental import pallas as pl
from jax.experimental.pallas import tpu as pltpu

D_MODEL = 2048
D_FF = 5632
HEAD_DIM = 128
N_HEADS = 8
WIDTH = N_HEADS * HEAD_DIM
CHUNK = 64
N_PAST_CHUNKS = 8
REL_CLIP_PAST = 128
N_REL = REL_CLIP_PAST + CHUNK
N_BRANCH = 2
IN_COLS = 6 * WIDTH + N_BRANCH * D_MODEL
EPS = 1e-6
SCALE = HEAD_DIM ** -0.5

COL_Q_SB, COL_K_SB, COL_V_SB, COL_Q_CA, COL_K_CA, COL_V_CA, COL_GATE = range(7)

V7X_VMEM_BYTES = 64 * 1024 * 1024
VMEM_LIMIT = V7X_VMEM_BYTES - 8 * 1024 * 1024

LOG_F32_UNDERFLOW = -104.0
LOG2_E = 1.4426950408889634
F32_SIGN_BIT = 0x80000000
MASK_BIAS = -1e30

F32 = jnp.float32
BF16 = jnp.bfloat16
BF16_SUBLANES = 16


def _dot(a, b):
    return jnp.dot(a, b, preferred_element_type=F32)


def _dot_nt(a, b):
    return lax.dot_general(a, b, (((1,), (1,)), ((), ())), preferred_element_type=F32)


def _rmsnorm_f32(x, gain):
    return x * lax.rsqrt(jnp.mean(x * x, axis=-1, keepdims=True) + EPS) * gain


def _ffn_kernel(x_ref, nin_ref, wg_ref, wu_ref, wd_ref, nout_ref, *rest, emit_residual):
    if emit_residual:
        o_ref, hn_out_ref, hn_scr = rest
    else:
        o_ref, hn_scr = rest
    f = pl.program_id(1)

    @pl.when(f == 0)
    def _():
        x = x_ref[...]
        hn_scr[...] = _rmsnorm_f32(x, nin_ref[...]).astype(BF16)
        o_ref[...] = x

    h = hn_scr[...]
    g = _dot(h, wg_ref[...])
    u = _dot(h, wu_ref[...])
    a = (g * u * 0.5) / (1.0 + jnp.exp(-g))
    o_ref[...] += _dot(a.astype(BF16), wd_ref[...])

    @pl.when(f == pl.num_programs(1) - 1)
    def _():
        y = _rmsnorm_f32(o_ref[...], nout_ref[...])
        if emit_residual:
            hn_out_ref[...] = y.astype(BF16)
        else:
            o_ref[...] = y


def _ffn(x, norm_in, wg, wu, wd, norm_out, *, emit_residual, tm=512, tf=512):
    m = x.shape[0]
    grid = (m // tm, D_FF // tf)
    wg, wu, wd = wg.astype(BF16), wu.astype(BF16), wd.astype(BF16)
    out_shape = [jax.ShapeDtypeStruct((m, D_MODEL), F32)]
    out_specs = [pl.BlockSpec((tm, D_MODEL), lambda i, f: (i, 0))]
    if emit_residual:
        out_shape.append(jax.ShapeDtypeStruct((m, D_MODEL), BF16))
        out_specs.append(pl.BlockSpec((tm, D_MODEL), lambda i, f: (i, 0)))
    res = pl.pallas_call(
        functools.partial(_ffn_kernel, emit_residual=emit_residual),
        grid=grid,
        in_specs=[
            pl.BlockSpec((tm, D_MODEL), lambda i, f: (i, 0)),
            pl.BlockSpec((1, D_MODEL), lambda i, f: (0, 0)),
            pl.BlockSpec((D_MODEL, tf), lambda i, f: (0, f)),
            pl.BlockSpec((D_MODEL, tf), lambda i, f: (0, f)),
            pl.BlockSpec((tf, D_MODEL), lambda i, f: (f, 0)),
            pl.BlockSpec((1, D_MODEL), lambda i, f: (0, 0)),
        ],
        out_specs=out_specs,
        out_shape=out_shape,
        scratch_shapes=[pltpu.VMEM((tm, D_MODEL), BF16)],
        compiler_params=pltpu.CompilerParams(
            dimension_semantics=("parallel", "arbitrary"), vmem_limit_bytes=VMEM_LIMIT),
        name="ffn_res" if emit_residual else "ffn_final",
    )(x, norm_in, wg, wu, wd, norm_out)
    return res


def _inproj_kernel(h_ref, w_ref, *rest, epilogue, n_side):
    rest = list(rest)
    aux_ref = None if epilogue == "plain" else rest.pop(0)
    side_in = [rest.pop(0) for _ in range(n_side)]
    o_ref = rest.pop(0)
    side_out = [rest.pop(0) for _ in range(n_side)]
    (w_scr,) = rest

    @pl.when(pl.program_id(1) == 0)
    def _():
        w_scr[...] = w_ref[...].astype(BF16)

    for src, dst in zip(side_in, side_out):
        dst[...] = src[...].astype(BF16)

    r = _dot(h_ref[...], w_scr[...])
    if epilogue == "plain":
        o_ref[...] = r.astype(BF16)
    elif epilogue == "head_norm":
        gains = aux_ref[pl.ds(pl.program_id(0), 1), :]
        for hd in range(N_HEADS):
            sl = slice(hd * HEAD_DIM, (hd + 1) * HEAD_DIM)
            o_ref[:, sl] = _rmsnorm_f32(r[:, sl], gains).astype(BF16)
    else:
        assert epilogue == "gate"
        o_ref[...] = (1.0 / (1.0 + jnp.exp(-(r + aux_ref[...])))).astype(BF16)


def _inproj(hn, w_in, w_col_blocks, epilogue, aux=None, *, side_casts=(), tm=1024):
    m = hn.shape[0]
    n = len(w_col_blocks)
    row_tiles = m // tm
    steps = n * row_tiles
    first, last = w_col_blocks[0], w_col_blocks[-1]
    assert list(w_col_blocks[:-1]) == list(range(first, first + n - 1)) and last >= first + n - 1
    assert (aux is None) == (epilogue == "plain")
    aux_specs = {
        "plain": [],
        "gate": [pl.BlockSpec((1, WIDTH), lambda j, i: (0, j))],
        "head_norm": [pl.BlockSpec((n, HEAD_DIM), lambda j, i: (0, 0))],
    }[epilogue]
    side_specs = []
    for a in side_casts:
        rows, cols = a.shape
        slab = rows // steps
        assert slab * steps == rows and slab % BF16_SUBLANES == 0, (a.shape, steps)
        side_specs.append(pl.BlockSpec((slab, cols), lambda j, i: (j * row_tiles + i, 0)))
    res = pl.pallas_call(
        functools.partial(_inproj_kernel, epilogue=epilogue, n_side=len(side_casts)),
        grid=(n, row_tiles),
        in_specs=[
            pl.BlockSpec((tm, D_MODEL), lambda j, i: (i, 0)),
            pl.BlockSpec((D_MODEL, WIDTH), lambda j, i: (0, first + j + (last - first - n + 1) * (j // (n - 1)))),
            *aux_specs,
            *side_specs,
        ],
        out_specs=[pl.BlockSpec((tm, WIDTH), lambda j, i: (i, j)), *side_specs],
        out_shape=[jax.ShapeDtypeStruct((m, n * WIDTH), BF16),
                   *[jax.ShapeDtypeStruct(a.shape, BF16) for a in side_casts]],
        scratch_shapes=[pltpu.VMEM((D_MODEL, WIDTH), BF16)],
        compiler_params=pltpu.CompilerParams(
            dimension_semantics=("arbitrary", "arbitrary"), vmem_limit_bytes=VMEM_LIMIT),
        name="inproj_" + epilogue,
    )(hn, w_in, *([] if aux is None else [aux]), *side_casts)
    return res[0], res[1:]


SB_BLK = 256
SB_HEADS = 4


def _sb_kernel(q_ref, k_ref, v_ref, o_ref):
    blk = SB_BLK
    row = lax.broadcasted_iota(jnp.int32, (blk, blk), 0)
    col = lax.broadcasted_iota(jnp.int32, (blk, blk), 1)
    later_sum = (row > col).astype(BF16)
    causal = col < row

    def sweep_block(qs, kb, state, diagonal):
        start = pl.multiple_of(kb * blk, blk)
        log_keeps, log_betas, splits = [], [], []
        for hd in range(SB_HEADS):
            lanes = slice(hd * HEAD_DIM, (hd + 1) * HEAD_DIM)
            nz = _dot_nt(qs[hd], k_ref[pl.ds(start, blk), lanes]) * (-SCALE * LOG2_E)
            neg_abs = pltpu.bitcast(pltpu.bitcast(nz, jnp.uint32) | jnp.uint32(F32_SIGN_BIT), F32)
            log_keep = jnp.minimum(nz, 0.0) - jnp.log2(1.0 + jnp.exp2(neg_abs))
            log_betas.append(log_keep - nz)
            if diagonal:
                log_keep = jnp.where(causal, log_keep, 0.0)
            log_keeps.append(log_keep)
            hi = log_keep.astype(BF16)
            splits += [hi, (log_keep - hi.astype(F32)).astype(BF16)]
        later_all = _dot(jnp.concatenate(splits, axis=0), later_sum)
        new_state = []
        for hd in range(SB_HEADS):
            run, acc = state[hd]
            lanes = slice(hd * HEAD_DIM, (hd + 1) * HEAD_DIM)
            later = later_all[2 * hd * blk:(2 * hd + 1) * blk] + later_all[(2 * hd + 1) * blk:(2 * hd + 2) * blk]
            w = jnp.exp2(log_betas[hd] + later + run)
            if diagonal:
                w = jnp.where(causal, w, 0.0)
            acc = acc + _dot(w.astype(BF16), v_ref[pl.ds(start, blk), lanes])
            run = run + jnp.sum(log_keeps[hd], axis=1, keepdims=True)
            new_state.append((run, acc))
        return tuple(new_state)

    def query_tile(qi, first):
        q0 = pl.multiple_of(qi * blk, blk)
        qs = [q_ref[pl.ds(q0, blk), hd * HEAD_DIM:(hd + 1) * HEAD_DIM] for hd in range(SB_HEADS)]
        run0 = jnp.zeros((blk, 1), F32)
        acc0 = jnp.zeros((blk, HEAD_DIM), F32)
        state = sweep_block(qs, qi, ((run0, acc0),) * SB_HEADS, True)
        if not first:
            state = sweep_block(qs, qi - 1, state, False)

            def cond(c):
                kb, st = c
                worst = functools.reduce(jnp.maximum, [run for run, _ in st])
                return jnp.logical_and(kb >= 0, jnp.max(worst) > LOG_F32_UNDERFLOW * LOG2_E)

            def body(c):
                kb, st = c
                return kb - 1, sweep_block(qs, kb, st, False)

            _, state = lax.while_loop(cond, body, (qi - 2, state))
        for hd in range(SB_HEADS):
            o_ref[pl.ds(q0, blk), hd * HEAD_DIM:(hd + 1) * HEAD_DIM] = state[hd][1].astype(BF16)

    query_tile(jnp.int32(0), True)
    lax.fori_loop(1, q_ref.shape[0] // blk, lambda qi, c: (query_tile(qi, False), c)[1], 0)


def _sb_attention(qkv, q_col, k_col, v_col):
    b, s, _ = qkv.shape
    groups = N_HEADS // SB_HEADS
    blk = (None, s, SB_HEADS * HEAD_DIM)

    def col(c):
        return lambda bi, g: (bi, 0, c * groups + g)

    return pl.pallas_call(
        _sb_kernel,
        grid=(b, groups),
        in_specs=[pl.BlockSpec(blk, col(q_col)), pl.BlockSpec(blk, col(k_col)), pl.BlockSpec(blk, col(v_col))],
        out_specs=pl.BlockSpec(blk, col(0)),
        out_shape=jax.ShapeDtypeStruct((b, s, WIDTH), BF16),
        compiler_params=pltpu.CompilerParams(
            dimension_semantics=("parallel", "parallel"), vmem_limit_bytes=VMEM_LIMIT),
        name="sb_attn",
    )(qkv, qkv, qkv)


CA_TQ = 128
CA_PAST = N_PAST_CHUNKS * CHUNK
CA_WIN = CA_PAST + CA_TQ
CA_ROLL = 1024
CA_GROUP = 4


def _ca_bias_row(rel_bias):
    far = N_PAST_CHUNKS * CHUNK - REL_CLIP_PAST + 1
    past = jnp.broadcast_to(rel_bias[:, N_REL - 1:], (N_HEADS, far))
    ramp = rel_bias[:, N_REL - 2::-1]
    future = jnp.broadcast_to(rel_bias[:, :1], (N_HEADS, CA_WIN + 1 - far - (N_REL - 1)))
    wrap = jnp.broadcast_to(rel_bias[:, N_REL - 1:], (N_HEADS, CA_ROLL - CA_WIN - 1))
    return jnp.concatenate([past, ramp, future, wrap], axis=1).astype(F32)[:, None, :]


def _ca_kernel(q_ref, k_ref, v_ref, brow_ref, o_ref, bias_scr):
    @pl.when(pl.program_id(1) == 0)
    def _():
        rows = jnp.broadcast_to(brow_ref[...], (CA_TQ, CA_ROLL))
        table = pltpu.roll(rows, 0, 1, stride=1, stride_axis=0)[:, :CA_WIN]
        q_chunk = lax.broadcasted_iota(jnp.int32, (CA_TQ, CA_WIN), 0) // CHUNK
        k_chunk = lax.broadcasted_iota(jnp.int32, (CA_TQ, CA_WIN), 1) // CHUNK
        in_band = (k_chunk >= q_chunk) & (k_chunk <= q_chunk + N_PAST_CHUNKS)
        bias_scr[...] = jnp.where(in_band, table * LOG2_E, MASK_BIAS)

    def query_tiles(tiles):
        scores = [_dot_nt(q_ref[pl.ds(q0, CA_TQ), :], k_ref[pl.ds(k0, n), :]) * (SCALE * LOG2_E)
                  + bias_scr[:, CA_WIN - n:] for q0, k0, n in tiles]
        probs = [jnp.exp2(s - jnp.max(s, axis=1, keepdims=True)) for s in scores]
        outs = [_dot(p.astype(BF16), v_ref[pl.ds(k0, n), :]) for p, (_, k0, n) in zip(probs, tiles)]
        for p, o, (q0, _, _) in zip(probs, outs, tiles):
            o_ref[pl.ds(q0, CA_TQ), :] = (o / jnp.sum(p, axis=1, keepdims=True)).astype(BF16)

    n_head = CA_PAST // CA_TQ
    query_tiles([(t * CA_TQ, 0, (t + 1) * CA_TQ) for t in range(n_head)])

    def full_tiles(g, carry):
        q0 = pl.multiple_of((n_head + g * CA_GROUP) * CA_TQ, CA_TQ)
        query_tiles([(q0 + j * CA_TQ, q0 + j * CA_TQ - CA_PAST, CA_WIN) for j in range(CA_GROUP)])
        return carry

    n_tiles = q_ref.shape[0] // CA_TQ
    assert (n_tiles - n_head) % CA_GROUP == 0
    lax.fori_loop(0, (n_tiles - n_head) // CA_GROUP, full_tiles, 0)


def _ca_attention(qk, v, v_col, bias_row):
    b, s, _ = qk.shape
    blk = (None, s, HEAD_DIM)

    def col(c):
        return lambda h, bi: (bi, 0, c * N_HEADS + h)

    return pl.pallas_call(
        _ca_kernel,
        grid=(N_HEADS, b),
        in_specs=[
            pl.BlockSpec(blk, col(0)),
            pl.BlockSpec(blk, col(1)),
            pl.BlockSpec(blk, col(v_col)),
            pl.BlockSpec((None, 1, CA_ROLL), lambda h, bi: (h, 0, 0)),
        ],
        out_specs=pl.BlockSpec(blk, lambda h, bi: (bi, 0, h)),
        out_shape=jax.ShapeDtypeStruct((b, s, WIDTH), BF16),
        scratch_shapes=[pltpu.VMEM((CA_TQ, CA_WIN), F32)],
        compiler_params=pltpu.CompilerParams(
            dimension_semantics=("arbitrary", "arbitrary"), vmem_limit_bytes=VMEM_LIMIT),
        name="ca_attn",
    )(qk, qk, v, bias_row)


def _merge_kernel(x_ref, a_ref, c_ref, ga_ref, gb_ref, woa_ref, wob_ref, wout_ref, o_ref):
    ya = _dot(a_ref[...], woa_ref[...])
    yb = _dot(c_ref[...], wob_ref[...])
    merged = ga_ref[...].astype(F32) * ya + gb_ref[...].astype(F32) * yb
    o_ref[...] = x_ref[...] + _dot(merged.astype(BF16), wout_ref[...])


def _merge(x1, att_sb, att_ca, gates, w_o_sb, w_o_ca, w_out, *, tm=512):
    m = x1.shape[0]
    resident = dict(pipeline_mode=pl.Buffered(1))
    return pl.pallas_call(
        _merge_kernel,
        grid=(m // tm,),
        in_specs=[
            pl.BlockSpec((tm, D_MODEL), lambda i: (i, 0)),
            pl.BlockSpec((tm, WIDTH), lambda i: (i, 0)),
            pl.BlockSpec((tm, WIDTH), lambda i: (i, 0)),
            pl.BlockSpec((tm, D_MODEL), lambda i: (i, 0)),
            pl.BlockSpec((tm, D_MODEL), lambda i: (i, 1)),
            pl.BlockSpec((WIDTH, D_MODEL), lambda i: (0, 0), **resident),
            pl.BlockSpec((WIDTH, D_MODEL), lambda i: (0, 0), **resident),
            pl.BlockSpec((D_MODEL, D_MODEL), lambda i: (0, 0), **resident),
        ],
        out_specs=pl.BlockSpec((tm, D_MODEL), lambda i: (i, 0)),
        out_shape=jax.ShapeDtypeStruct((m, D_MODEL), F32),
        compiler_params=pltpu.CompilerParams(
            dimension_semantics=("parallel",), vmem_limit_bytes=VMEM_LIMIT),
        name="merge",
    )(x1, att_sb, att_ca, gates, gates, w_o_sb, w_o_ca, w_out)


def kernel(x, ffn1_norm, ffn1_w_gate, ffn1_w_up, ffn1_w_down, mix_norm, w_in, b_gate,
           q_norm_ca, k_norm_ca, rel_bias, w_o_sb, w_o_ca, w_out, ffn2_norm,
           ffn2_w_gate, ffn2_w_up, ffn2_w_down, final_norm):
    b, s, d = x.shape
    depth = ffn1_norm.shape[0]
    row = lambda v: v.reshape(1, -1)
    xf = x.reshape(b * s, d)
    for l in range(depth):
        x1, hn = _ffn(xf, row(ffn1_norm[l]), ffn1_w_gate[l], ffn1_w_up[l], ffn1_w_down[l],
                      row(mix_norm[l]), emit_residual=True)
        qkv, (wg2,) = _inproj(hn, w_in[l], (COL_Q_SB, COL_K_SB, COL_V_SB, COL_V_CA), "plain",
                              side_casts=(ffn2_w_gate[l],))
        qk_ca, (wd2, wo_sb, wo_ca, wo) = _inproj(
            hn, w_in[l], (COL_Q_CA, COL_K_CA), "head_norm", jnp.stack([q_norm_ca[l], k_norm_ca[l]]),
            side_casts=(ffn2_w_down[l], w_o_sb[l], w_o_ca[l], w_out[l]))
        gates, (wu2,) = _inproj(hn, w_in[l], tuple(range(COL_GATE, IN_COLS // WIDTH)), "gate", row(b_gate[l]),
                                side_casts=(ffn2_w_up[l],))
        qkv3 = qkv.reshape(b, s, -1)
        att_sb = _sb_attention(qkv3, 0, 1, 2).reshape(b * s, WIDTH)
        att_ca = _ca_attention(qk_ca.reshape(b, s, -1), qkv3, 3, _ca_bias_row(rel_bias[l])).reshape(b * s, WIDTH)
        x2 = _merge(x1, att_sb, att_ca, gates, wo_sb, wo_ca, wo)
        (xf,) = _ffn(x2, row(ffn2_norm[l]), wg2, wu2, wd2, row(final_norm[l]), emit_residual=False)
    return xf.reshape(b, s, d)
```

```python
import functools

import jax
import jax.numpy as jnp
from jax import lax
from jax.experimental import pallas as pl
from jax.experimental.pallas import tpu as pltpu

D_MODEL = 2048
D_FF = 5632
HEAD_DIM = 128
N_HEADS = 8
WIDTH = N_HEADS * HEAD_DIM
CHUNK = 64
N_PAST_CHUNKS = 8
REL_CLIP_PAST = 128
N_REL = REL_CLIP_PAST + CHUNK
N_BRANCH = 2
IN_COLS = 6 * WIDTH + N_BRANCH * D_MODEL
EPS = 1e-6
SCALE = HEAD_DIM ** -0.5

COL_Q_SB, COL_K_SB, COL_V_SB, COL_Q_CA, COL_K_CA, COL_V_CA, COL_GATE = range(7)

V7X_VMEM_BYTES = 64 * 1024 * 1024
VMEM_LIMIT = V7X_VMEM_BYTES - 8 * 1024 * 1024

LOG_F32_UNDERFLOW = -104.0
LOG2_E = 1.4426950408889634
MASK_BIAS = -1e30

F32 = jnp.float32
BF16 = jnp.bfloat16
BF16_SUBLANES = 16


def _dot(a, b):
    return jnp.dot(a, b, preferred_element_type=F32)


def _dot_nt(a, b):
    return lax.dot_general(a, b, (((1,), (1,)), ((), ())), preferred_element_type=F32)


def _rmsnorm_f32(x, gain):
    return x * lax.rsqrt(jnp.mean(x * x, axis=-1, keepdims=True) + EPS) * gain


def _ffn_kernel(x_ref, nin_ref, wg_ref, wu_ref, wd_ref, nout_ref, *rest, emit_residual):
    if emit_residual:
        o_ref, hn_out_ref, hn_scr = rest
    else:
        o_ref, hn_scr = rest
    f = pl.program_id(1)

    @pl.when(f == 0)
    def _():
        x = x_ref[...]
        hn_scr[...] = _rmsnorm_f32(x, nin_ref[...]).astype(BF16)
        o_ref[...] = x

    h = hn_scr[...]
    g = _dot(h, wg_ref[...])
    u = _dot(h, wu_ref[...])
    a = (g * u * 0.5) / (1.0 + jnp.exp(-g))
    o_ref[...] += _dot(a.astype(BF16), wd_ref[...])

    @pl.when(f == pl.num_programs(1) - 1)
    def _():
        y = _rmsnorm_f32(o_ref[...], nout_ref[...])
        if emit_residual:
            hn_out_ref[...] = y.astype(BF16)
        else:
            o_ref[...] = y


def _ffn(x, norm_in, wg, wu, wd, norm_out, *, emit_residual, tm=512, tf=512):
    m = x.shape[0]
    grid = (m // tm, D_FF // tf)
    wg, wu, wd = wg.astype(BF16), wu.astype(BF16), wd.astype(BF16)
    out_shape = [jax.ShapeDtypeStruct((m, D_MODEL), F32)]
    out_specs = [pl.BlockSpec((tm, D_MODEL), lambda i, f: (i, 0))]
    if emit_residual:
        out_shape.append(jax.ShapeDtypeStruct((m, D_MODEL), BF16))
        out_specs.append(pl.BlockSpec((tm, D_MODEL), lambda i, f: (i, 0)))
    res = pl.pallas_call(
        functools.partial(_ffn_kernel, emit_residual=emit_residual),
        grid=grid,
        in_specs=[
            pl.BlockSpec((tm, D_MODEL), lambda i, f: (i, 0)),
            pl.BlockSpec((1, D_MODEL), lambda i, f: (0, 0)),
            pl.BlockSpec((D_MODEL, tf), lambda i, f: (0, f)),
            pl.BlockSpec((D_MODEL, tf), lambda i, f: (0, f)),
            pl.BlockSpec((tf, D_MODEL), lambda i, f: (f, 0)),
            pl.BlockSpec((1, D_MODEL), lambda i, f: (0, 0)),
        ],
        out_specs=out_specs,
        out_shape=out_shape,
        scratch_shapes=[pltpu.VMEM((tm, D_MODEL), BF16)],
        compiler_params=pltpu.CompilerParams(
            dimension_semantics=("parallel", "arbitrary"), vmem_limit_bytes=VMEM_LIMIT),
        name="ffn_res" if emit_residual else "ffn_final",
    )(x, norm_in, wg, wu, wd, norm_out)
    return res


def _inproj_kernel(h_ref, w_ref, *rest, epilogue, n_side):
    rest = list(rest)
    aux_ref = None if epilogue == "plain" else rest.pop(0)
    side_in = [rest.pop(0) for _ in range(n_side)]
    o_ref = rest.pop(0)
    side_out = [rest.pop(0) for _ in range(n_side)]
    (w_scr,) = rest

    @pl.when(pl.program_id(1) == 0)
    def _():
        w_scr[...] = w_ref[...].astype(BF16)

    for src, dst in zip(side_in, side_out):
        dst[...] = src[...].astype(BF16)

    r = _dot(h_ref[...], w_scr[...])
    if epilogue == "plain":
        o_ref[...] = r.astype(BF16)
    elif epilogue == "head_norm":
        gains = aux_ref[pl.ds(pl.program_id(0), 1), :]
        for hd in range(N_HEADS):
            sl = slice(hd * HEAD_DIM, (hd + 1) * HEAD_DIM)
            o_ref[:, sl] = _rmsnorm_f32(r[:, sl], gains).astype(BF16)
    else:
        assert epilogue == "gate"
        o_ref[...] = (1.0 / (1.0 + jnp.exp(-(r + aux_ref[...])))).astype(BF16)


def _inproj(hn, w_in, w_col_blocks, epilogue, aux=None, *, side_casts=(), tm=1024):
    m = hn.shape[0]
    n = len(w_col_blocks)
    row_tiles = m // tm
    steps = n * row_tiles
    first, last = w_col_blocks[0], w_col_blocks[-1]
    assert list(w_col_blocks[:-1]) == list(range(first, first + n - 1)) and last >= first + n - 1
    assert (aux is None) == (epilogue == "plain")
    aux_specs = {
        "plain": [],
        "gate": [pl.BlockSpec((1, WIDTH), lambda j, i: (0, j))],
        "head_norm": [pl.BlockSpec((n, HEAD_DIM), lambda j, i: (0, 0))],
    }[epilogue]
    side_specs = []
    for a in side_casts:
        rows, cols = a.shape
        slab = rows // steps
        assert slab * steps == rows and slab % BF16_SUBLANES == 0, (a.shape, steps)
        side_specs.append(pl.BlockSpec((slab, cols), lambda j, i: (j * row_tiles + i, 0)))
    res = pl.pallas_call(
        functools.partial(_inproj_kernel, epilogue=epilogue, n_side=len(side_casts)),
        grid=(n, row_tiles),
        in_specs=[
            pl.BlockSpec((tm, D_MODEL), lambda j, i: (i, 0)),
            pl.BlockSpec((D_MODEL, WIDTH), lambda j, i: (0, first + j + (last - first - n + 1) * (j // (n - 1)))),
            *aux_specs,
            *side_specs,
        ],
        out_specs=[pl.BlockSpec((tm, WIDTH), lambda j, i: (i, j)), *side_specs],
        out_shape=[jax.ShapeDtypeStruct((m, n * WIDTH), BF16),
                   *[jax.ShapeDtypeStruct(a.shape, BF16) for a in side_casts]],
        scratch_shapes=[pltpu.VMEM((D_MODEL, WIDTH), BF16)],
        compiler_params=pltpu.CompilerParams(
            dimension_semantics=("arbitrary", "arbitrary"), vmem_limit_bytes=VMEM_LIMIT),
        name="inproj_" + epilogue,
    )(hn, w_in, *([] if aux is None else [aux]), *side_casts)
    return res[0], res[1:]


SB_BLK = 256
SB_HEADS = 4


def _sb_kernel(q_ref, k_ref, v_ref, o_ref):
    blk = SB_BLK
    row = lax.broadcasted_iota(jnp.int32, (blk, blk), 0)
    col = lax.broadcasted_iota(jnp.int32, (blk, blk), 1)
    later_sum = (row > col).astype(BF16)
    causal = col < row

    def sweep_block(qs, kb, state, diagonal):
        start = pl.multiple_of(kb * blk, blk)
        log_keeps, log_betas, splits = [], [], []
        for hd in range(SB_HEADS):
            lanes = slice(hd * HEAD_DIM, (hd + 1) * HEAD_DIM)
            nz = _dot_nt(qs[hd], k_ref[pl.ds(start, blk), lanes]) * (-SCALE * LOG2_E)
            log_keep = jnp.minimum(nz, 0.0) - jnp.log2(1.0 + jnp.exp2(-jnp.abs(nz)))
            log_betas.append(log_keep - nz)
            if diagonal:
                log_keep = jnp.where(causal, log_keep, 0.0)
            log_keeps.append(log_keep)
            hi = log_keep.astype(BF16)
            splits += [hi, (log_keep - hi.astype(F32)).astype(BF16)]
        later_all = _dot(jnp.concatenate(splits, axis=0), later_sum)
        new_state = []
        for hd in range(SB_HEADS):
            run, acc = state[hd]
            lanes = slice(hd * HEAD_DIM, (hd + 1) * HEAD_DIM)
            later = later_all[2 * hd * blk:(2 * hd + 1) * blk] + later_all[(2 * hd + 1) * blk:(2 * hd + 2) * blk]
            w = jnp.exp2(log_betas[hd] + later + run)
            if diagonal:
                w = jnp.where(causal, w, 0.0)
            acc = acc + _dot(w.astype(BF16), v_ref[pl.ds(start, blk), lanes])
            run = run + jnp.sum(log_keeps[hd], axis=1, keepdims=True)
            new_state.append((run, acc))
        return tuple(new_state)

    def query_tile(qi, first):
        q0 = pl.multiple_of(qi * blk, blk)
        qs = [q_ref[pl.ds(q0, blk), hd * HEAD_DIM:(hd + 1) * HEAD_DIM] for hd in range(SB_HEADS)]
        run0 = jnp.zeros((blk, 1), F32)
        acc0 = jnp.zeros((blk, HEAD_DIM), F32)
        state = sweep_block(qs, qi, ((run0, acc0),) * SB_HEADS, True)
        if not first:
            state = sweep_block(qs, qi - 1, state, False)

            def cond(c):
                kb, st = c
                worst = functools.reduce(jnp.maximum, [run for run, _ in st])
                return jnp.logical_and(kb >= 0, jnp.max(worst) > LOG_F32_UNDERFLOW * LOG2_E)

            def body(c):
                kb, st = c
                return kb - 1, sweep_block(qs, kb, st, False)

            _, state = lax.while_loop(cond, body, (qi - 2, state))
        for hd in range(SB_HEADS):
            o_ref[pl.ds(q0, blk), hd * HEAD_DIM:(hd + 1) * HEAD_DIM] = state[hd][1].astype(BF16)

    query_tile(jnp.int32(0), True)
    lax.fori_loop(1, q_ref.shape[0] // blk, lambda qi, c: (query_tile(qi, False), c)[1], 0)


def _sb_attention(qkv, q_col, k_col, v_col):
    b, s, _ = qkv.shape
    groups = N_HEADS // SB_HEADS
    blk = (None, s, SB_HEADS * HEAD_DIM)

    def col(c):
        return lambda bi, g: (bi, 0, c * groups + g)

    return pl.pallas_call(
        _sb_kernel,
        grid=(b, groups),
        in_specs=[pl.BlockSpec(blk, col(q_col)), pl.BlockSpec(blk, col(k_col)), pl.BlockSpec(blk, col(v_col))],
        out_specs=pl.BlockSpec(blk, col(0)),
        out_shape=jax.ShapeDtypeStruct((b, s, WIDTH), BF16),
        compiler_params=pltpu.CompilerParams(
            dimension_semantics=("parallel", "parallel"), vmem_limit_bytes=VMEM_LIMIT),
        name="sb_attn",
    )(qkv, qkv, qkv)


CA_TQ = 128
CA_PAST = N_PAST_CHUNKS * CHUNK
CA_WIN = CA_PAST + CA_TQ
CA_ROLL = 1024
CA_GROUP = 4


def _ca_bias_row(rel_bias):
    far = N_PAST_CHUNKS * CHUNK - REL_CLIP_PAST + 1
    past = jnp.broadcast_to(rel_bias[:, N_REL - 1:], (N_HEADS, far))
    ramp = rel_bias[:, N_REL - 2::-1]
    future = jnp.broadcast_to(rel_bias[:, :1], (N_HEADS, CA_WIN + 1 - far - (N_REL - 1)))
    wrap = jnp.broadcast_to(rel_bias[:, N_REL - 1:], (N_HEADS, CA_ROLL - CA_WIN - 1))
    return jnp.concatenate([past, ramp, future, wrap], axis=1).astype(F32)[:, None, :]


def _ca_kernel(q_ref, k_ref, v_ref, brow_ref, o_ref, bias_scr):
    @pl.when(pl.program_id(1) == 0)
    def _():
        rows = jnp.broadcast_to(brow_ref[...], (CA_TQ, CA_ROLL))
        table = pltpu.roll(rows, 0, 1, stride=1, stride_axis=0)[:, :CA_WIN]
        q_chunk = lax.broadcasted_iota(jnp.int32, (CA_TQ, CA_WIN), 0) // CHUNK
        k_chunk = lax.broadcasted_iota(jnp.int32, (CA_TQ, CA_WIN), 1) // CHUNK
        in_band = (k_chunk >= q_chunk) & (k_chunk <= q_chunk + N_PAST_CHUNKS)
        bias_scr[...] = jnp.where(in_band, table * LOG2_E, MASK_BIAS)

    def query_tiles(tiles):
        scores = [_dot_nt(q_ref[pl.ds(q0, CA_TQ), :], k_ref[pl.ds(k0, n), :]) * (SCALE * LOG2_E)
                  + bias_scr[:, CA_WIN - n:] for q0, k0, n in tiles]
        probs = [jnp.exp2(s - jnp.max(s, axis=1, keepdims=True)) for s in scores]
        outs = [_dot(p.astype(BF16), v_ref[pl.ds(k0, n), :]) for p, (_, k0, n) in zip(probs, tiles)]
        for p, o, (q0, _, _) in zip(probs, outs, tiles):
            o_ref[pl.ds(q0, CA_TQ), :] = (o / jnp.sum(p, axis=1, keepdims=True)).astype(BF16)

    n_head = CA_PAST // CA_TQ
    query_tiles([(t * CA_TQ, 0, (t + 1) * CA_TQ) for t in range(n_head)])

    def full_tiles(g, carry):
        q0 = pl.multiple_of((n_head + g * CA_GROUP) * CA_TQ, CA_TQ)
        query_tiles([(q0 + j * CA_TQ, q0 + j * CA_TQ - CA_PAST, CA_WIN) for j in range(CA_GROUP)])
        return carry

    n_tiles = q_ref.shape[0] // CA_TQ
    assert (n_tiles - n_head) % CA_GROUP == 0
    lax.fori_loop(0, (n_tiles - n_head) // CA_GROUP, full_tiles, 0)


def _ca_attention(qk, v, v_col, bias_row):
    b, s, _ = qk.shape
    blk = (None, s, HEAD_DIM)

    def col(c):
        return lambda h, bi: (bi, 0, c * N_HEADS + h)

    return pl.pallas_call(
        _ca_kernel,
        grid=(N_HEADS, b),
        in_specs=[
            pl.BlockSpec(blk, col(0)),
            pl.BlockSpec(blk, col(1)),
            pl.BlockSpec(blk, col(v_col)),
            pl.BlockSpec((None, 1, CA_ROLL), lambda h, bi: (h, 0, 0)),
        ],
        out_specs=pl.BlockSpec(blk, lambda h, bi: (bi, 0, h)),
        out_shape=jax.ShapeDtypeStruct((b, s, WIDTH), BF16),
        scratch_shapes=[pltpu.VMEM((CA_TQ, CA_WIN), F32)],
        compiler_params=pltpu.CompilerParams(
            dimension_semantics=("arbitrary", "arbitrary"), vmem_limit_bytes=VMEM_LIMIT),
        name="ca_attn",
    )(qk, qk, v, bias_row)


def _merge_kernel(x_ref, a_ref, c_ref, ga_ref, gb_ref, woa_ref, wob_ref, wout_ref, o_ref):
    ya = _dot(a_ref[...], woa_ref[...])
    yb = _dot(c_ref[...], wob_ref[...])
    merged = ga_ref[...].astype(F32) * ya + gb_ref[...].astype(F32) * yb
    o_ref[...] = x_ref[...] + _dot(merged.astype(BF16), wout_ref[...])


def _merge(x1, att_sb, att_ca, gates, w_o_sb, w_o_ca, w_out, *, tm=512):
    m = x1.shape[0]
    resident = dict(pipeline_mode=pl.Buffered(1))
    return pl.pallas_call(
        _merge_kernel,
        grid=(m // tm,),
        in_specs=[
            pl.BlockSpec((tm, D_MODEL), lambda i: (i, 0)),
            pl.BlockSpec((tm, WIDTH), lambda i: (i, 0)),
            pl.BlockSpec((tm, WIDTH), lambda i: (i, 0)),
            pl.BlockSpec((tm, D_MODEL), lambda i: (i, 0)),
            pl.BlockSpec((tm, D_MODEL), lambda i: (i, 1)),
            pl.BlockSpec((WIDTH, D_MODEL), lambda i: (0, 0), **resident),
            pl.BlockSpec((WIDTH, D_MODEL), lambda i: (0, 0), **resident),
            pl.BlockSpec((D_MODEL, D_MODEL), lambda i: (0, 0), **resident),
        ],
        out_specs=pl.BlockSpec((tm, D_MODEL), lambda i: (i, 0)),
        out_shape=jax.ShapeDtypeStruct((m, D_MODEL), F32),
        compiler_params=pltpu.CompilerParams(
            dimension_semantics=("parallel",), vmem_limit_bytes=VMEM_LIMIT),
        name="merge",
    )(x1, att_sb, att_ca, gates, gates, w_o_sb, w_o_ca, w_out)


def kernel(x, ffn1_norm, ffn1_w_gate, ffn1_w_up, ffn1_w_down, mix_norm, w_in, b_gate,
           q_norm_ca, k_norm_ca, rel_bias, w_o_sb, w_o_ca, w_out, ffn2_norm,
           ffn2_w_gate, ffn2_w_up, ffn2_w_down, final_norm):
    b, s, d = x.shape
    depth = ffn1_norm.shape[0]
    row = lambda v: v.reshape(1, -1)
    xf = x.reshape(b * s, d)
    for l in range(depth):
        x1, hn = _ffn(xf, row(ffn1_norm[l]), ffn1_w_gate[l], ffn1_w_up[l], ffn1_w_down[l],
                      row(mix_norm[l]), emit_residual=True)
        qkv, (wg2,) = _inproj(hn, w_in[l], (COL_Q_SB, COL_K_SB, COL_V_SB, COL_V_CA), "plain",
                              side_casts=(ffn2_w_gate[l],))
        qk_ca, (wd2, wo_sb, wo_ca, wo) = _inproj(
            hn, w_in[l], (COL_Q_CA, COL_K_CA), "head_norm", jnp.stack([q_norm_ca[l], k_norm_ca[l]]),
            side_casts=(ffn2_w_down[l], w_o_sb[l], w_o_ca[l], w_out[l]))
        gates, (wu2,) = _inproj(hn, w_in[l], tuple(range(COL_GATE, IN_COLS // WIDTH)), "gate", row(b_gate[l]),
                                side_casts=(ffn2_w_up[l],))
        qkv3 = qkv.reshape(b, s, -1)
        att_sb = _sb_attention(qkv3, 0, 1, 2).reshape(b * s, WIDTH)
        att_ca = _ca_attention(qk_ca.reshape(b, s, -1), qkv3, 3, _ca_bias_row(rel_bias[l])).reshape(b * s, WIDTH)
        x2 = _merge(x1, att_sb, att_ca, gates, wo_sb, wo_ca, wo)
        (xf,) = _ffn(x2, row(ffn2_norm[l]), wg2, wu2, wd2, row(final_norm[l]), emit_residual=False)
    return xf.reshape(b, s, d)
```

```python
import functools

import jax
import jax.numpy as jnp
from jax import lax
from jax.experimental import pallas as pl
from jax.experimental.pallas import tpu as pltpu

D_MODEL = 2048
D_FF = 5632
HEAD_DIM = 128
N_HEADS = 8
WIDTH = N_HEADS * HEAD_DIM
CHUNK = 64
N_PAST_CHUNKS = 8
REL_CLIP_PAST = 128
N_REL = REL_CLIP_PAST + CHUNK
N_BRANCH = 2
IN_COLS = 6 * WIDTH + N_BRANCH * D_MODEL
EPS = 1e-6
SCALE = HEAD_DIM ** -0.5

COL_Q_SB, COL_K_SB, COL_V_SB, COL_Q_CA, COL_K_CA, COL_V_CA, COL_GATE = range(7)

V7X_VMEM_BYTES = 64 * 1024 * 1024
VMEM_LIMIT = V7X_VMEM_BYTES - 8 * 1024 * 1024

LOG_F32_UNDERFLOW = -104.0
LOG2_E = 1.4426950408889634
MASK_BIAS = -1e30

F32 = jnp.float32
BF16 = jnp.bfloat16
BF16_SUBLANES = 16


def _dot(a, b):
    return jnp.dot(a, b, preferred_element_type=F32)


def _dot_nt(a, b):
    return lax.dot_general(a, b, (((1,), (1,)), ((), ())), preferred_element_type=F32)


def _rmsnorm_f32(x, gain):
    return x * lax.rsqrt(jnp.mean(x * x, axis=-1, keepdims=True) + EPS) * gain


def _ffn_kernel(x_ref, nin_ref, wg_ref, wu_ref, wd_ref, nout_ref, *rest, emit_residual):
    if emit_residual:
        o_ref, hn_out_ref, hn_scr = rest
    else:
        o_ref, hn_scr = rest
    f = pl.program_id(1)

    @pl.when(f == 0)
    def _():
        x = x_ref[...]
        hn_scr[...] = _rmsnorm_f32(x, nin_ref[...]).astype(BF16)
        o_ref[...] = x

    h = hn_scr[...]
    g = _dot(h, wg_ref[...])
    u = _dot(h, wu_ref[...])
    a = (g * u * 0.5) / (1.0 + jnp.exp(-g))
    o_ref[...] += _dot(a.astype(BF16), wd_ref[...])

    @pl.when(f == pl.num_programs(1) - 1)
    def _():
        y = _rmsnorm_f32(o_ref[...], nout_ref[...])
        if emit_residual:
            hn_out_ref[...] = y.astype(BF16)
        else:
            o_ref[...] = y


def _ffn(x, norm_in, wg, wu, wd, norm_out, *, emit_residual, tm=512, tf=512):
    m = x.shape[0]
    grid = (m // tm, D_FF // tf)
    wg, wu, wd = wg.astype(BF16), wu.astype(BF16), wd.astype(BF16)
    out_shape = [jax.ShapeDtypeStruct((m, D_MODEL), F32)]
    out_specs = [pl.BlockSpec((tm, D_MODEL), lambda i, f: (i, 0))]
    if emit_residual:
        out_shape.append(jax.ShapeDtypeStruct((m, D_MODEL), BF16))
        out_specs.append(pl.BlockSpec((tm, D_MODEL), lambda i, f: (i, 0)))
    res = pl.pallas_call(
        functools.partial(_ffn_kernel, emit_residual=emit_residual),
        grid=grid,
        in_specs=[
            pl.BlockSpec((tm, D_MODEL), lambda i, f: (i, 0)),
            pl.BlockSpec((1, D_MODEL), lambda i, f: (0, 0)),
            pl.BlockSpec((D_MODEL, tf), lambda i, f: (0, f)),
            pl.BlockSpec((D_MODEL, tf), lambda i, f: (0, f)),
            pl.BlockSpec((tf, D_MODEL), lambda i, f: (f, 0)),
            pl.BlockSpec((1, D_MODEL), lambda i, f: (0, 0)),
        ],
        out_specs=out_specs,
        out_shape=out_shape,
        scratch_shapes=[pltpu.VMEM((tm, D_MODEL), BF16)],
        compiler_params=pltpu.CompilerParams(
            dimension_semantics=("parallel", "arbitrary"), vmem_limit_bytes=VMEM_LIMIT),
        name="ffn_res" if emit_residual else "ffn_final",
    )(x, norm_in, wg, wu, wd, norm_out)
    return res


def _inproj_kernel(h_ref, w_ref, *rest, epilogue, n_side):
    rest = list(rest)
    aux_ref = None if epilogue == "plain" else rest.pop(0)
    side_in = [rest.pop(0) for _ in range(n_side)]
    o_ref = rest.pop(0)
    side_out = [rest.pop(0) for _ in range(n_side)]
    (w_scr,) = rest

    @pl.when(pl.program_id(1) == 0)
    def _():
        w_scr[...] = w_ref[...].astype(BF16)

    for src, dst in zip(side_in, side_out):
        dst[...] = src[...].astype(BF16)

    r = _dot(h_ref[...], w_scr[...])
    if epilogue == "plain":
        o_ref[...] = r.astype(BF16)
    elif epilogue == "head_norm":
        gains = aux_ref[pl.ds(pl.program_id(0), 1), :]
        for hd in range(N_HEADS):
            sl = slice(hd * HEAD_DIM, (hd + 1) * HEAD_DIM)
            o_ref[:, sl] = _rmsnorm_f32(r[:, sl], gains).astype(BF16)
    else:
        assert epilogue == "gate"
        o_ref[...] = (0.5 * jnp.tanh(0.5 * (r + aux_ref[...])) + 0.5).astype(BF16)


def _inproj(hn, w_in, w_col_blocks, epilogue, aux=None, *, side_casts=(), tm=1024):
    m = hn.shape[0]
    n = len(w_col_blocks)
    row_tiles = m // tm
    steps = n * row_tiles
    first, last = w_col_blocks[0], w_col_blocks[-1]
    assert list(w_col_blocks[:-1]) == list(range(first, first + n - 1)) and last >= first + n - 1
    assert (aux is None) == (epilogue == "plain")
    aux_specs = {
        "plain": [],
        "gate": [pl.BlockSpec((1, WIDTH), lambda j, i: (0, j))],
        "head_norm": [pl.BlockSpec((n, HEAD_DIM), lambda j, i: (0, 0))],
    }[epilogue]
    side_specs = []
    for a in side_casts:
        rows, cols = a.shape
        slab = rows // steps
        assert slab * steps == rows and slab % BF16_SUBLANES == 0, (a.shape, steps)
        side_specs.append(pl.BlockSpec((slab, cols), lambda j, i: (j * row_tiles + i, 0)))
    res = pl.pallas_call(
        functools.partial(_inproj_kernel, epilogue=epilogue, n_side=len(side_casts)),
        grid=(n, row_tiles),
        in_specs=[
            pl.BlockSpec((tm, D_MODEL), lambda j, i: (i, 0)),
            pl.BlockSpec((D_MODEL, WIDTH), lambda j, i: (0, first + j + (last - first - n + 1) * (j // (n - 1)))),
            *aux_specs,
            *side_specs,
        ],
        out_specs=[pl.BlockSpec((tm, WIDTH), lambda j, i: (i, j)), *side_specs],
        out_shape=[jax.ShapeDtypeStruct((m, n * WIDTH), BF16),
                   *[jax.ShapeDtypeStruct(a.shape, BF16) for a in side_casts]],
        scratch_shapes=[pltpu.VMEM((D_MODEL, WIDTH), BF16)],
        compiler_params=pltpu.CompilerParams(
            dimension_semantics=("arbitrary", "arbitrary"), vmem_limit_bytes=VMEM_LIMIT),
        name="inproj_" + epilogue,
    )(hn, w_in, *([] if aux is None else [aux]), *side_casts)
    return res[0], res[1:]


SB_BLK = 256
SB_HEADS = 4


def _sb_kernel(q_ref, k_ref, v_ref, o_ref):
    blk = SB_BLK
    row = lax.broadcasted_iota(jnp.int32, (blk, blk), 0)
    col = lax.broadcasted_iota(jnp.int32, (blk, blk), 1)
    later_sum = (row > col).astype(BF16)
    causal = col < row

    def sweep_block(qs, kb, state, diagonal):
        start = pl.multiple_of(kb * blk, blk)
        log_keeps, log_betas, splits = [], [], []
        for hd in range(SB_HEADS):
            lanes = slice(hd * HEAD_DIM, (hd + 1) * HEAD_DIM)
            nz = _dot_nt(qs[hd], k_ref[pl.ds(start, blk), lanes]) * (-SCALE * LOG2_E)
            if diagonal:
                nz = jnp.where(causal, nz, -MASK_BIAS)
            log_keep = jnp.minimum(nz, 0.0) - jnp.log2(1.0 + jnp.exp2(-jnp.abs(nz)))
            log_betas.append(log_keep - nz)
            log_keeps.append(log_keep)
            hi = log_keep.astype(BF16)
            splits += [hi, (log_keep - hi.astype(F32)).astype(BF16)]
        later_all = _dot(jnp.concatenate(splits, axis=0), later_sum)
        new_state = []
        for hd in range(SB_HEADS):
            run, acc = state[hd]
            lanes = slice(hd * HEAD_DIM, (hd + 1) * HEAD_DIM)
            later = later_all[2 * hd * blk:(2 * hd + 1) * blk] + later_all[(2 * hd + 1) * blk:(2 * hd + 2) * blk]
            w = jnp.exp2(log_betas[hd] + later + run)
            acc = acc + _dot(w.astype(BF16), v_ref[pl.ds(start, blk), lanes])
            run = run + jnp.sum(log_keeps[hd], axis=1, keepdims=True)
            new_state.append((run, acc))
        return tuple(new_state)

    def query_tile(qi, first):
        q0 = pl.multiple_of(qi * blk, blk)
        qs = [q_ref[pl.ds(q0, blk), hd * HEAD_DIM:(hd + 1) * HEAD_DIM] for hd in range(SB_HEADS)]
        run0 = jnp.zeros((blk, 1), F32)
        acc0 = jnp.zeros((blk, HEAD_DIM), F32)
        state = sweep_block(qs, qi, ((run0, acc0),) * SB_HEADS, True)
        if not first:
            state = sweep_block(qs, qi - 1, state, False)

            def cond(c):
                kb, st = c
                worst = functools.reduce(jnp.maximum, [run for run, _ in st])
                return jnp.logical_and(kb >= 0, jnp.max(worst) > LOG_F32_UNDERFLOW * LOG2_E)

            def body(c):
                kb, st = c
                return kb - 1, sweep_block(qs, kb, st, False)

            _, state = lax.while_loop(cond, body, (qi - 2, state))
        for hd in range(SB_HEADS):
            o_ref[pl.ds(q0, blk), hd * HEAD_DIM:(hd + 1) * HEAD_DIM] = state[hd][1].astype(BF16)

    query_tile(jnp.int32(0), True)
    lax.fori_loop(1, q_ref.shape[0] // blk, lambda qi, c: (query_tile(qi, False), c)[1], 0)


def _sb_attention(qkv, q_col, k_col, v_col):
    b, s, _ = qkv.shape
    groups = N_HEADS // SB_HEADS
    blk = (None, s, SB_HEADS * HEAD_DIM)

    def col(c):
        return lambda bi, g: (bi, 0, c * groups + g)

    return pl.pallas_call(
        _sb_kernel,
        grid=(b, groups),
        in_specs=[pl.BlockSpec(blk, col(q_col)), pl.BlockSpec(blk, col(k_col)), pl.BlockSpec(blk, col(v_col))],
        out_specs=pl.BlockSpec(blk, col(0)),
        out_shape=jax.ShapeDtypeStruct((b, s, WIDTH), BF16),
        compiler_params=pltpu.CompilerParams(
            dimension_semantics=("parallel", "parallel"), vmem_limit_bytes=VMEM_LIMIT),
        name="sb_attn",
    )(qkv, qkv, qkv)


CA_TQ = 128
CA_PAST = N_PAST_CHUNKS * CHUNK
CA_WIN = CA_PAST + CA_TQ
CA_ROLL = 1024
CA_GROUP = 4


def _ca_bias_row(rel_bias):
    far = N_PAST_CHUNKS * CHUNK - REL_CLIP_PAST + 1
    past = jnp.broadcast_to(rel_bias[:, N_REL - 1:], (N_HEADS, far))
    ramp = rel_bias[:, N_REL - 2::-1]
    future = jnp.broadcast_to(rel_bias[:, :1], (N_HEADS, CA_WIN + 1 - far - (N_REL - 1)))
    wrap = jnp.broadcast_to(rel_bias[:, N_REL - 1:], (N_HEADS, CA_ROLL - CA_WIN - 1))
    return jnp.concatenate([past, ramp, future, wrap], axis=1).astype(F32)[:, None, :]


def _ca_kernel(q_ref, k_ref, v_ref, brow_ref, o_ref, bias_scr):
    @pl.when(pl.program_id(1) == 0)
    def _():
        rows = jnp.broadcast_to(brow_ref[...], (CA_TQ, CA_ROLL))
        table = pltpu.roll(rows, 0, 1, stride=1, stride_axis=0)[:, :CA_WIN]
        q_chunk = lax.broadcasted_iota(jnp.int32, (CA_TQ, CA_WIN), 0) // CHUNK
        k_chunk = lax.broadcasted_iota(jnp.int32, (CA_TQ, CA_WIN), 1) // CHUNK
        in_band = (k_chunk >= q_chunk) & (k_chunk <= q_chunk + N_PAST_CHUNKS)
        bias_scr[...] = jnp.where(in_band, table * LOG2_E, MASK_BIAS)

    def query_tiles(tiles):
        scores = [_dot_nt(q_ref[pl.ds(q0, CA_TQ), :], k_ref[pl.ds(k0, n), :]) * (SCALE * LOG2_E)
                  + bias_scr[:, CA_WIN - n:] for q0, k0, n in tiles]
        probs = [jnp.exp2(s - jnp.max(s, axis=1, keepdims=True)) for s in scores]
        outs = [_dot(p.astype(BF16), v_ref[pl.ds(k0, n), :]) for p, (_, k0, n) in zip(probs, tiles)]
        for p, o, (q0, _, _) in zip(probs, outs, tiles):
            o_ref[pl.ds(q0, CA_TQ), :] = (o / jnp.sum(p, axis=1, keepdims=True)).astype(BF16)

    n_head = CA_PAST // CA_TQ
    query_tiles([(t * CA_TQ, 0, (t + 1) * CA_TQ) for t in range(n_head)])

    def full_tiles(g, carry):
        q0 = pl.multiple_of((n_head + g * CA_GROUP) * CA_TQ, CA_TQ)
        query_tiles([(q0 + j * CA_TQ, q0 + j * CA_TQ - CA_PAST, CA_WIN) for j in range(CA_GROUP)])
        return carry

    n_tiles = q_ref.shape[0] // CA_TQ
    assert (n_tiles - n_head) % CA_GROUP == 0
    lax.fori_loop(0, (n_tiles - n_head) // CA_GROUP, full_tiles, 0)


def _ca_attention(qk, v, v_col, bias_row):
    b, s, _ = qk.shape
    blk = (None, s, HEAD_DIM)

    def col(c):
        return lambda h, bi: (bi, 0, c * N_HEADS + h)

    return pl.pallas_call(
        _ca_kernel,
        grid=(N_HEADS, b),
        in_specs=[
            pl.BlockSpec(blk, col(0)),
            pl.BlockSpec(blk, col(1)),
            pl.BlockSpec(blk, col(v_col)),
            pl.BlockSpec((None, 1, CA_ROLL), lambda h, bi: (h, 0, 0)),
        ],
        out_specs=pl.BlockSpec(blk, lambda h, bi: (bi, 0, h)),
        out_shape=jax.ShapeDtypeStruct((b, s, WIDTH), BF16),
        scratch_shapes=[pltpu.VMEM((CA_TQ, CA_WIN), F32)],
        compiler_params=pltpu.CompilerParams(
            dimension_semantics=("arbitrary", "arbitrary"), vmem_limit_bytes=VMEM_LIMIT),
        name="ca_attn",
    )(qk, qk, v, bias_row)


def _merge_kernel(x_ref, a_ref, c_ref, ga_ref, gb_ref, woa_ref, wob_ref, wout_ref, o_ref):
    ya = _dot(a_ref[...], woa_ref[...])
    yb = _dot(c_ref[...], wob_ref[...])
    merged = ga_ref[...].astype(F32) * ya + gb_ref[...].astype(F32) * yb
    o_ref[...] = x_ref[...] + _dot(merged.astype(BF16), wout_ref[...])


def _merge(x1, att_sb, att_ca, gates, w_o_sb, w_o_ca, w_out, *, tm=512):
    m = x1.shape[0]
    resident = dict(pipeline_mode=pl.Buffered(1))
    return pl.pallas_call(
        _merge_kernel,
        grid=(m // tm,),
        in_specs=[
            pl.BlockSpec((tm, D_MODEL), lambda i: (i, 0)),
            pl.BlockSpec((tm, WIDTH), lambda i: (i, 0)),
            pl.BlockSpec((tm, WIDTH), lambda i: (i, 0)),
            pl.BlockSpec((tm, D_MODEL), lambda i: (i, 0)),
            pl.BlockSpec((tm, D_MODEL), lambda i: (i, 1)),
            pl.BlockSpec((WIDTH, D_MODEL), lambda i: (0, 0), **resident),
            pl.BlockSpec((WIDTH, D_MODEL), lambda i: (0, 0), **resident),
            pl.BlockSpec((D_MODEL, D_MODEL), lambda i: (0, 0), **resident),
        ],
        out_specs=pl.BlockSpec((tm, D_MODEL), lambda i: (i, 0)),
        out_shape=jax.ShapeDtypeStruct((m, D_MODEL), F32),
        compiler_params=pltpu.CompilerParams(
            dimension_semantics=("parallel",), vmem_limit_bytes=VMEM_LIMIT),
        name="merge",
    )(x1, att_sb, att_ca, gates, gates, w_o_sb, w_o_ca, w_out)


def kernel(x, ffn1_norm, ffn1_w_gate, ffn1_w_up, ffn1_w_down, mix_norm, w_in, b_gate,
           q_norm_ca, k_norm_ca, rel_bias, w_o_sb, w_o_ca, w_out, ffn2_norm,
           ffn2_w_gate, ffn2_w_up, ffn2_w_down, final_norm):
    b, s, d = x.shape
    depth = ffn1_norm.shape[0]
    row = lambda v: v.reshape(1, -1)
    xf = x.reshape(b * s, d)
    for l in range(depth):
        x1, hn = _ffn(xf, row(ffn1_norm[l]), ffn1_w_gate[l], ffn1_w_up[l], ffn1_w_down[l],
                      row(mix_norm[l]), emit_residual=True)
        qkv, (wg2,) = _inproj(hn, w_in[l], (COL_Q_SB, COL_K_SB, COL_V_SB, COL_V_CA), "plain",
                              side_casts=(ffn2_w_gate[l],))
        qk_ca, (wd2, wo_sb, wo_ca, wo) = _inproj(
            hn, w_in[l], (COL_Q_CA, COL_K_CA), "head_norm", jnp.stack([q_norm_ca[l], k_norm_ca[l]]),
            side_casts=(ffn2_w_down[l], w_o_sb[l], w_o_ca[l], w_out[l]))
        gates, (wu2,) = _inproj(hn, w_in[l], tuple(range(COL_GATE, IN_COLS // WIDTH)), "gate", row(b_gate[l]),
                                side_casts=(ffn2_w_up[l],))
        qkv3 = qkv.reshape(b, s, -1)
        att_sb = _sb_attention(qkv3, 0, 1, 2).reshape(b * s, WIDTH)
        att_ca = _ca_attention(qk_ca.reshape(b, s, -1), qkv3, 3, _ca_bias_row(rel_bias[l])).reshape(b * s, WIDTH)
        x2 = _merge(x1, att_sb, att_ca, gates, wo_sb, wo_ca, wo)
        (xf,) = _ffn(x2, row(ffn2_norm[l]), wg2, wu2, wd2, row(final_norm[l]), emit_residual=False)
    return xf.reshape(b, s, d)
```

```python
import functools

import jax
import jax.numpy as jnp
from jax import lax
from jax.experimental import pallas as pl
from jax.experimental.pallas import tpu as pltpu

D_MODEL = 2048
D_FF = 5632
HEAD_DIM = 128
N_HEADS = 8
WIDTH = N_HEADS * HEAD_DIM
CHUNK = 64
N_PAST_CHUNKS = 8
REL_CLIP_PAST = 128
N_REL = REL_CLIP_PAST + CHUNK
N_BRANCH = 2
IN_COLS = 6 * WIDTH + N_BRANCH * D_MODEL
EPS = 1e-6
SCALE = HEAD_DIM ** -0.5

COL_Q_SB, COL_K_SB, COL_V_SB, COL_Q_CA, COL_K_CA, COL_V_CA, COL_GATE = range(7)

V7X_VMEM_BYTES = 64 * 1024 * 1024
VMEM_LIMIT = V7X_VMEM_BYTES - 8 * 1024 * 1024

LOG_F32_UNDERFLOW = -104.0
LOG2_E = 1.4426950408889634
MASK_BIAS = -1e30

F32 = jnp.float32
BF16 = jnp.bfloat16
BF16_SUBLANES = 16


def _dot(a, b):
    return jnp.dot(a, b, preferred_element_type=F32)


def _dot_nt(a, b):
    return lax.dot_general(a, b, (((1,), (1,)), ((), ())), preferred_element_type=F32)


def _rmsnorm_f32(x, gain):
    return x * lax.rsqrt(jnp.mean(x * x, axis=-1, keepdims=True) + EPS) * gain


def _ffn_kernel(x_ref, nin_ref, wg_ref, wu_ref, wd_ref, nout_ref, *rest, emit_residual):
    if emit_residual:
        o_ref, hn_out_ref, hn_scr = rest
    else:
        o_ref, hn_scr = rest
    f = pl.program_id(1)

    @pl.when(f == 0)
    def _():
        x = x_ref[...]
        hn_scr[...] = _rmsnorm_f32(x, nin_ref[...]).astype(BF16)
        o_ref[...] = x

    h = hn_scr[...]
    g = _dot(h, wg_ref[...])
    u = _dot(h, wu_ref[...])
    a = (g * u * 0.5) / (1.0 + jnp.exp(-g))
    o_ref[...] += _dot(a.astype(BF16), wd_ref[...])

    @pl.when(f == pl.num_programs(1) - 1)
    def _():
        y = _rmsnorm_f32(o_ref[...], nout_ref[...])
        if emit_residual:
            hn_out_ref[...] = y.astype(BF16)
        else:
            o_ref[...] = y


def _ffn(x, norm_in, wg, wu, wd, norm_out, *, emit_residual, tm=512, tf=512):
    m = x.shape[0]
    grid = (m // tm, D_FF // tf)
    wg, wu, wd = wg.astype(BF16), wu.astype(BF16), wd.astype(BF16)
    out_shape = [jax.ShapeDtypeStruct((m, D_MODEL), F32)]
    out_specs = [pl.BlockSpec((tm, D_MODEL), lambda i, f: (i, 0))]
    if emit_residual:
        out_shape.append(jax.ShapeDtypeStruct((m, D_MODEL), BF16))
        out_specs.append(pl.BlockSpec((tm, D_MODEL), lambda i, f: (i, 0)))
    res = pl.pallas_call(
        functools.partial(_ffn_kernel, emit_residual=emit_residual),
        grid=grid,
        in_specs=[
            pl.BlockSpec((tm, D_MODEL), lambda i, f: (i, 0)),
            pl.BlockSpec((1, D_MODEL), lambda i, f: (0, 0)),
            pl.BlockSpec((D_MODEL, tf), lambda i, f: (0, f)),
            pl.BlockSpec((D_MODEL, tf), lambda i, f: (0, f)),
            pl.BlockSpec((tf, D_MODEL), lambda i, f: (f, 0)),
            pl.BlockSpec((1, D_MODEL), lambda i, f: (0, 0)),
        ],
        out_specs=out_specs,
        out_shape=out_shape,
        scratch_shapes=[pltpu.VMEM((tm, D_MODEL), BF16)],
        compiler_params=pltpu.CompilerParams(
            dimension_semantics=("parallel", "arbitrary"), vmem_limit_bytes=VMEM_LIMIT),
        name="ffn_res" if emit_residual else "ffn_final",
    )(x, norm_in, wg, wu, wd, norm_out)
    return res


def _inproj_kernel(h_ref, w_ref, *rest, epilogue, n_side):
    rest = list(rest)
    aux_ref = None if epilogue == "plain" else rest.pop(0)
    side_in = [rest.pop(0) for _ in range(n_side)]
    o_ref = rest.pop(0)
    side_out = [rest.pop(0) for _ in range(n_side)]
    (w_scr,) = rest

    @pl.when(pl.program_id(1) == 0)
    def _():
        w_scr[...] = w_ref[...].astype(BF16)

    for src, dst in zip(side_in, side_out):
        dst[...] = src[...].astype(BF16)

    r = _dot(h_ref[...], w_scr[...])
    if epilogue == "plain":
        o_ref[...] = r.astype(BF16)
    elif epilogue == "head_norm":
        gains = aux_ref[pl.ds(pl.program_id(0), 1), :]
        for hd in range(N_HEADS):
            sl = slice(hd * HEAD_DIM, (hd + 1) * HEAD_DIM)
            o_ref[:, sl] = _rmsnorm_f32(r[:, sl], gains).astype(BF16)
    else:
        assert epilogue == "gate"
        o_ref[...] = (0.5 * jnp.tanh(0.5 * (r + aux_ref[...])) + 0.5).astype(BF16)


def _inproj(hn, w_in, w_col_blocks, epilogue, aux=None, *, side_casts=(), tm=1024):
    m = hn.shape[0]
    n = len(w_col_blocks)
    row_tiles = m // tm
    steps = n * row_tiles
    first, last = w_col_blocks[0], w_col_blocks[-1]
    assert list(w_col_blocks[:-1]) == list(range(first, first + n - 1)) and last >= first + n - 1
    assert (aux is None) == (epilogue == "plain")
    aux_specs = {
        "plain": [],
        "gate": [pl.BlockSpec((1, WIDTH), lambda j, i: (0, j))],
        "head_norm": [pl.BlockSpec((n, HEAD_DIM), lambda j, i: (0, 0))],
    }[epilogue]
    side_specs = []
    for a in side_casts:
        rows, cols = a.shape
        slab = rows // steps
        assert slab * steps == rows and slab % BF16_SUBLANES == 0, (a.shape, steps)
        side_specs.append(pl.BlockSpec((slab, cols), lambda j, i: (j * row_tiles + i, 0)))
    res = pl.pallas_call(
        functools.partial(_inproj_kernel, epilogue=epilogue, n_side=len(side_casts)),
        grid=(n, row_tiles),
        in_specs=[
            pl.BlockSpec((tm, D_MODEL), lambda j, i: (i, 0)),
            pl.BlockSpec((D_MODEL, WIDTH), lambda j, i: (0, first + j + (last - first - n + 1) * (j // (n - 1)))),
            *aux_specs,
            *side_specs,
        ],
        out_specs=[pl.BlockSpec((tm, WIDTH), lambda j, i: (i, j)), *side_specs],
        out_shape=[jax.ShapeDtypeStruct((m, n * WIDTH), BF16),
                   *[jax.ShapeDtypeStruct(a.shape, BF16) for a in side_casts]],
        scratch_shapes=[pltpu.VMEM((D_MODEL, WIDTH), BF16)],
        compiler_params=pltpu.CompilerParams(
            dimension_semantics=("arbitrary", "arbitrary"), vmem_limit_bytes=VMEM_LIMIT),
        name="inproj_" + epilogue,
    )(hn, w_in, *([] if aux is None else [aux]), *side_casts)
    return res[0], res[1:]


SB_BLK = 256
SB_HEADS = 4


def _sb_kernel(q_ref, k_ref, v_ref, o_ref):
    blk = SB_BLK
    row = lax.broadcasted_iota(jnp.int32, (blk, blk), 0)
    col = lax.broadcasted_iota(jnp.int32, (blk, blk), 1)
    later_sum = (row > col).astype(BF16)
    causal = col < row

    def sweep_block(qs, kb, state, diagonal):
        start = pl.multiple_of(kb * blk, blk)
        log_keeps, log_betas, splits = [], [], []
        for hd in range(SB_HEADS):
            lanes = slice(hd * HEAD_DIM, (hd + 1) * HEAD_DIM)
            nz = _dot_nt(qs[hd], k_ref[pl.ds(start, blk), lanes]) * (-SCALE * LOG2_E)
            if diagonal:
                nz = jnp.where(causal, nz, -MASK_BIAS)
            log_keep = jnp.minimum(nz, 0.0) - jnp.log2(1.0 + jnp.exp2(-jnp.abs(nz)))
            log_betas.append(log_keep - nz)
            log_keeps.append(log_keep)
            hi = log_keep.astype(BF16)
            splits += [hi, (log_keep - hi.astype(F32)).astype(BF16)]
        later_all = _dot(jnp.concatenate(splits, axis=0), later_sum)
        new_state = []
        for hd in range(SB_HEADS):
            run, acc = state[hd]
            lanes = slice(hd * HEAD_DIM, (hd + 1) * HEAD_DIM)
            later = later_all[2 * hd * blk:(2 * hd + 1) * blk] + later_all[(2 * hd + 1) * blk:(2 * hd + 2) * blk]
            w = jnp.exp2(log_betas[hd] + later + run)
            acc = acc + _dot(w.astype(BF16), v_ref[pl.ds(start, blk), lanes])
            run = run + jnp.sum(log_keeps[hd], axis=1, keepdims=True)
            new_state.append((run, acc))
        return tuple(new_state)

    def query_tile(qi, first):
        q0 = pl.multiple_of(qi * blk, blk)
        qs = [q_ref[pl.ds(q0, blk), hd * HEAD_DIM:(hd + 1) * HEAD_DIM] for hd in range(SB_HEADS)]
        run0 = jnp.zeros((blk, 1), F32)
        acc0 = jnp.zeros((blk, HEAD_DIM), F32)
        state = sweep_block(qs, qi, ((run0, acc0),) * SB_HEADS, True)
        if not first:
            state = sweep_block(qs, qi - 1, state, False)

            def cond(c):
                kb, st = c
                worst = functools.reduce(jnp.maximum, [run for run, _ in st])
                return jnp.logical_and(kb >= 0, jnp.max(worst) > LOG_F32_UNDERFLOW * LOG2_E)

            def body(c):
                kb, st = c
                return kb - 1, sweep_block(qs, kb, st, False)

            _, state = lax.while_loop(cond, body, (qi - 2, state))
        for hd in range(SB_HEADS):
            o_ref[pl.ds(q0, blk), hd * HEAD_DIM:(hd + 1) * HEAD_DIM] = state[hd][1].astype(BF16)

    query_tile(jnp.int32(0), True)
    lax.fori_loop(1, q_ref.shape[0] // blk, lambda qi, c: (query_tile(qi, False), c)[1], 0)


def _sb_attention(qkv, q_col, k_col, v_col):
    b, s, _ = qkv.shape
    groups = N_HEADS // SB_HEADS
    blk = (None, s, SB_HEADS * HEAD_DIM)

    def col(c):
        return lambda bi, g: (bi, 0, c * groups + g)

    return pl.pallas_call(
        _sb_kernel,
        grid=(b, groups),
        in_specs=[pl.BlockSpec(blk, col(q_col)), pl.BlockSpec(blk, col(k_col)), pl.BlockSpec(blk, col(v_col))],
        out_specs=pl.BlockSpec(blk, col(0)),
        out_shape=jax.ShapeDtypeStruct((b, s, WIDTH), BF16),
        compiler_params=pltpu.CompilerParams(
            dimension_semantics=("parallel", "parallel"), vmem_limit_bytes=VMEM_LIMIT),
        name="sb_attn",
    )(qkv, qkv, qkv)


CA_TQ = 128
CA_PAST = N_PAST_CHUNKS * CHUNK
CA_WIN = CA_PAST + CA_TQ
CA_ROLL = 1024
CA_GROUP = 7


def _ca_bias_row(rel_bias):
    far = N_PAST_CHUNKS * CHUNK - REL_CLIP_PAST + 1
    past = jnp.broadcast_to(rel_bias[:, N_REL - 1:], (N_HEADS, far))
    ramp = rel_bias[:, N_REL - 2::-1]
    future = jnp.broadcast_to(rel_bias[:, :1], (N_HEADS, CA_WIN + 1 - far - (N_REL - 1)))
    wrap = jnp.broadcast_to(rel_bias[:, N_REL - 1:], (N_HEADS, CA_ROLL - CA_WIN - 1))
    return jnp.concatenate([past, ramp, future, wrap], axis=1).astype(F32)[:, None, :]


def _ca_kernel(q_ref, k_ref, v_ref, brow_ref, o_ref, bias_scr):
    @pl.when(pl.program_id(1) == 0)
    def _():
        rows = jnp.broadcast_to(brow_ref[...], (CA_TQ, CA_ROLL))
        table = pltpu.roll(rows, 0, 1, stride=1, stride_axis=0)[:, :CA_WIN]
        q_chunk = lax.broadcasted_iota(jnp.int32, (CA_TQ, CA_WIN), 0) // CHUNK
        k_chunk = lax.broadcasted_iota(jnp.int32, (CA_TQ, CA_WIN), 1) // CHUNK
        in_band = (k_chunk >= q_chunk) & (k_chunk <= q_chunk + N_PAST_CHUNKS)
        bias_scr[...] = jnp.where(in_band, table * LOG2_E, MASK_BIAS)

    def query_tiles(tiles):
        scores = [_dot_nt(q_ref[pl.ds(q0, CA_TQ), :], k_ref[pl.ds(k0, n), :]) * (SCALE * LOG2_E)
                  + bias_scr[:, CA_WIN - n:] for q0, k0, n in tiles]
        probs = [jnp.exp2(s - jnp.max(s, axis=1, keepdims=True)) for s in scores]
        outs = [_dot(p.astype(BF16), v_ref[pl.ds(k0, n), :]) for p, (_, k0, n) in zip(probs, tiles)]
        for p, o, (q0, _, _) in zip(probs, outs, tiles):
            o_ref[pl.ds(q0, CA_TQ), :] = (o / jnp.sum(p, axis=1, keepdims=True)).astype(BF16)

    n_head = CA_PAST // CA_TQ
    query_tiles([(t * CA_TQ, 0, (t + 1) * CA_TQ) for t in range(n_head)])

    def full_tiles(g, carry):
        q0 = pl.multiple_of((n_head + g * CA_GROUP) * CA_TQ, CA_TQ)
        query_tiles([(q0 + j * CA_TQ, q0 + j * CA_TQ - CA_PAST, CA_WIN) for j in range(CA_GROUP)])
        return carry

    n_tiles = q_ref.shape[0] // CA_TQ
    assert (n_tiles - n_head) % CA_GROUP == 0
    lax.fori_loop(0, (n_tiles - n_head) // CA_GROUP, full_tiles, 0)


def _ca_attention(qk, v, v_col, bias_row):
    b, s, _ = qk.shape
    blk = (None, s, HEAD_DIM)

    def col(c):
        return lambda h, bi: (bi, 0, c * N_HEADS + h)

    return pl.pallas_call(
        _ca_kernel,
        grid=(N_HEADS, b),
        in_specs=[
            pl.BlockSpec(blk, col(0)),
            pl.BlockSpec(blk, col(1)),
            pl.BlockSpec(blk, col(v_col)),
            pl.BlockSpec((None, 1, CA_ROLL), lambda h, bi: (h, 0, 0)),
        ],
        out_specs=pl.BlockSpec(blk, lambda h, bi: (bi, 0, h)),
        out_shape=jax.ShapeDtypeStruct((b, s, WIDTH), BF16),
        scratch_shapes=[pltpu.VMEM((CA_TQ, CA_WIN), F32)],
        compiler_params=pltpu.CompilerParams(
            dimension_semantics=("arbitrary", "arbitrary"), vmem_limit_bytes=VMEM_LIMIT),
        name="ca_attn",
    )(qk, qk, v, bias_row)


def _merge_kernel(x_ref, a_ref, c_ref, ga_ref, gb_ref, woa_ref, wob_ref, wout_ref, o_ref):
    ya = _dot(a_ref[...], woa_ref[...])
    yb = _dot(c_ref[...], wob_ref[...])
    merged = ga_ref[...].astype(F32) * ya + gb_ref[...].astype(F32) * yb
    o_ref[...] = x_ref[...] + _dot(merged.astype(BF16), wout_ref[...])


def _merge(x1, att_sb, att_ca, gates, w_o_sb, w_o_ca, w_out, *, tm=512):
    m = x1.shape[0]
    resident = dict(pipeline_mode=pl.Buffered(1))
    return pl.pallas_call(
        _merge_kernel,
        grid=(m // tm,),
        in_specs=[
            pl.BlockSpec((tm, D_MODEL), lambda i: (i, 0)),
            pl.BlockSpec((tm, WIDTH), lambda i: (i, 0)),
            pl.BlockSpec((tm, WIDTH), lambda i: (i, 0)),
            pl.BlockSpec((tm, D_MODEL), lambda i: (i, 0)),
            pl.BlockSpec((tm, D_MODEL), lambda i: (i, 1)),
            pl.BlockSpec((WIDTH, D_MODEL), lambda i: (0, 0), **resident),
            pl.BlockSpec((WIDTH, D_MODEL), lambda i: (0, 0), **resident),
            pl.BlockSpec((D_MODEL, D_MODEL), lambda i: (0, 0), **resident),
        ],
        out_specs=pl.BlockSpec((tm, D_MODEL), lambda i: (i, 0)),
        out_shape=jax.ShapeDtypeStruct((m, D_MODEL), F32),
        compiler_params=pltpu.CompilerParams(
            dimension_semantics=("parallel",), vmem_limit_bytes=VMEM_LIMIT),
        name="merge",
    )(x1, att_sb, att_ca, gates, gates, w_o_sb, w_o_ca, w_out)


def kernel(x, ffn1_norm, ffn1_w_gate, ffn1_w_up, ffn1_w_down, mix_norm, w_in, b_gate,
           q_norm_ca, k_norm_ca, rel_bias, w_o_sb, w_o_ca, w_out, ffn2_norm,
           ffn2_w_gate, ffn2_w_up, ffn2_w_down, final_norm):
    b, s, d = x.shape
    depth = ffn1_norm.shape[0]
    row = lambda v: v.reshape(1, -1)
    xf = x.reshape(b * s, d)
    for l in range(depth):
        x1, hn = _ffn(xf, row(ffn1_norm[l]), ffn1_w_gate[l], ffn1_w_up[l], ffn1_w_down[l],
                      row(mix_norm[l]), emit_residual=True)
        qkv, (wg2,) = _inproj(hn, w_in[l], (COL_Q_SB, COL_K_SB, COL_V_SB, COL_V_CA), "plain",
                              side_casts=(ffn2_w_gate[l],))
        qk_ca, (wd2, wo_sb, wo_ca, wo) = _inproj(
            hn, w_in[l], (COL_Q_CA, COL_K_CA), "head_norm", jnp.stack([q_norm_ca[l], k_norm_ca[l]]),
            side_casts=(ffn2_w_down[l], w_o_sb[l], w_o_ca[l], w_out[l]))
        gates, (wu2,) = _inproj(hn, w_in[l], tuple(range(COL_GATE, IN_COLS // WIDTH)), "gate", row(b_gate[l]),
                                side_casts=(ffn2_w_up[l],))
        qkv3 = qkv.reshape(b, s, -1)
        att_sb = _sb_attention(qkv3, 0, 1, 2).reshape(b * s, WIDTH)
        att_ca = _ca_attention(qk_ca.reshape(b, s, -1), qkv3, 3, _ca_bias_row(rel_bias[l])).reshape(b * s, WIDTH)
        x2 = _merge(x1, att_sb, att_ca, gates, wo_sb, wo_ca, wo)
        (xf,) = _ffn(x2, row(ffn2_norm[l]), wg2, wu2, wd2, row(final_norm[l]), emit_residual=False)
    return xf.reshape(b, s, d)
```
